```python
import jax
import jax.numpy as jnp
from jax import lax
import numpy as np

D_MODEL = 4096
BATCH = 32
SEQ = 256
DEPTH = 2
DEC_BATCH = 2
DEC_SEQ = 4096
PAST_LEN = 256

GRID_W = 64
BLOCK = 128
RMS_EPS = 1e-6
ROPE_THETA = 10000.0
N_MOD = 6
RET_HEADS = 16
RET_DK = D_MODEL // RET_HEADS
RET_DV = 2 * RET_DK
RET_QK = RET_HEADS * RET_DK
RET_V = RET_HEADS * RET_DV
RET_IN_DIM = 2 * RET_QK + 3 * RET_V
ATTN_HEADS = 32
ATTN_HEAD_DIM = D_MODEL // ATTN_HEADS
KV_HEADS = 8
GQA_GROUP = ATTN_HEADS // KV_HEADS
ATTN_Q = ATTN_HEADS * ATTN_HEAD_DIM
ATTN_KV = KV_HEADS * ATTN_HEAD_DIM
ATTN_QKV_DIM = ATTN_Q + 2 * ATTN_KV
FF_DENSE = 11008
N_EXPERTS = 8
TOP_K = 2
FF_EXPERT = 14336

kernel_name = 'hybrid_retention_gqa_prefix_dit_step'


def rms_norm(x, g):
    xf = x.astype(jnp.float32)
    y = xf * lax.rsqrt(jnp.mean(xf * xf, axis=-1, keepdims=True) + RMS_EPS)
    return (y * g.astype(jnp.float32)).astype(x.dtype)


def modulation(cond, w, b):
    m = jax.nn.silu(cond) @ w + b
    return jnp.split(m[:, None, :], N_MOD, axis=-1)


def modulate(x, g, shift, scale):
    return rms_norm(x, g) * (1 + scale) + shift


def axial_rope(x):
    t, d = x.shape[-2], x.shape[-1]
    rows = t // GRID_W
    half = d // 2
    n_freq = half // 2
    inv = ROPE_THETA ** (-jnp.arange(n_freq, dtype=jnp.float32) / n_freq)
    row = jnp.repeat(jnp.arange(rows, dtype=jnp.float32), GRID_W)
    col = jnp.tile(jnp.arange(GRID_W, dtype=jnp.float32), rows)
    ang = jnp.concatenate([row[:, None] * inv, col[:, None] * inv], axis=-1)
    cos, sin = jnp.cos(ang), jnp.sin(ang)
    xf = x.astype(jnp.float32)
    x1, x2 = xf[..., :half], xf[..., half:]
    return jnp.concatenate([x1 * cos - x2 * sin, x2 * cos + x1 * sin], axis=-1).astype(x.dtype)


def split_heads(x, n_heads):
    b, t, _ = x.shape
    return x.reshape(b, t, n_heads, -1).transpose(0, 2, 1, 3)


def merge_heads(x):
    b, h, t, d = x.shape
    return x.transpose(0, 2, 1, 3).reshape(b, t, h * d)


def retention_scan(q, k, v, log_gamma, s0):
    b, h, t, dk = q.shape
    dv = v.shape[-1]
    nc = t // BLOCK
    qc = q.reshape(b, h, nc, BLOCK, dk).transpose(2, 0, 1, 3, 4)
    kc = k.reshape(b, h, nc, BLOCK, dk).transpose(2, 0, 1, 3, 4)
    vc = v.reshape(b, h, nc, BLOCK, dv).transpose(2, 0, 1, 3, 4)
    idx = jnp.arange(BLOCK, dtype=jnp.float32)
    diff = idx[:, None] - idx[None, :]
    lg = log_gamma[:, None]
    intra = jnp.where(diff >= 0, jnp.exp(lg[:, :, None] * jnp.maximum(diff, 0.0)), 0.0)
    cross = jnp.exp(lg * (idx + 1.0))
    tail = jnp.exp(lg * (BLOCK - 1.0 - idx))
    chunk_decay = jnp.exp(log_gamma * BLOCK)

    def step(s, inp):
        qi, ki, vi = inp
        scores = jnp.einsum('bhid,bhjd->bhij', qi, ki) * intra[None]
        o = (jnp.einsum('bhij,bhjv->bhiv', scores, vi)
             + jnp.einsum('bhid,bhdv->bhiv', qi, s) * cross[None, :, :, None])
        s_new = (s * chunk_decay[None, :, None, None]
                 + jnp.einsum('bhjd,bhjv->bhdv', ki * tail[None, :, :, None], vi))
        return s_new, o

    s_fin, oc = lax.scan(step, s0, (qc, kc, vc))
    o = oc.transpose(1, 2, 0, 3, 4).reshape(b, h, t, dv)
    return o, s_fin


def head_group_norm(o):
    mu = jnp.mean(o, axis=-1, keepdims=True)
    oc = o - mu
    return oc * lax.rsqrt(jnp.mean(oc * oc, axis=-1, keepdims=True) + RMS_EPS)


def retention_mixer(h, w_in, decay_logit, w_out, init_state, latent):
    q, k, v, g_f, g_b = jnp.split(
        h @ w_in, [RET_QK, 2 * RET_QK, 2 * RET_QK + RET_V, 2 * RET_QK + 2 * RET_V], axis=-1)
    q = split_heads(q, RET_HEADS)
    k = split_heads(k, RET_HEADS)
    v = split_heads(v, RET_HEADS)
    if latent:
        q = axial_rope(q)
        k = axial_rope(k)
    q = q.astype(jnp.float32)
    k = k.astype(jnp.float32) * (RET_DK ** -0.5)
    v = v.astype(jnp.float32)
    log_gamma = jax.nn.log_sigmoid(decay_logit.astype(jnp.float32))
    s0 = init_state.astype(jnp.float32)
    o_f, s_f = retention_scan(q, k, v, log_gamma[0], s0[:, 0])
    o_b, s_b = retention_scan(q[:, :, ::-1], k[:, :, ::-1], v[:, :, ::-1], log_gamma[1], s0[:, 1])
    o_b = o_b[:, :, ::-1]
    y = (jax.nn.silu(g_f.astype(jnp.float32)) * merge_heads(head_group_norm(o_f))
         + jax.nn.silu(g_b.astype(jnp.float32)) * merge_heads(head_group_norm(o_b)))
    out = y.astype(h.dtype) @ w_out
    return out, jnp.stack([s_f, s_b], axis=1)


def blocked_attention(q, k, v):
    b, h, t, d = q.shape
    nb = t // BLOCK
    qb = q.reshape(b, KV_HEADS, GQA_GROUP, nb, BLOCK, d).transpose(3, 0, 1, 2, 4, 5)
    scale = d ** -0.5

    def one_block(qi):
        s = jnp.einsum('bkgqd,bknd->bkgqn', qi, k).astype(jnp.float32) * scale
        p = jax.nn.softmax(s, axis=-1)
        return jnp.einsum('bkgqn,bknd->bkgqd', p.astype(v.dtype), v)

    ob = lax.map(one_block, qb)
    return ob.transpose(1, 2, 3, 0, 4, 5).reshape(b, h, t, d)


def attention_mixer(h, w_qkv, q_norm, k_norm, w_o, ctx_k, ctx_v):
    q, k, v = jnp.split(h @ w_qkv, [ATTN_Q, ATTN_Q + ATTN_KV], axis=-1)
    q = rms_norm(split_heads(q, ATTN_HEADS), q_norm)
    k = rms_norm(split_heads(k, KV_HEADS), k_norm)
    v = split_heads(v, KV_HEADS)
    if ctx_k is None:
        k_all, v_all = k, v
    else:
        q = axial_rope(q)
        k = axial_rope(k)
        k_all = jnp.concatenate([ctx_k.astype(k.dtype), k], axis=2)
        v_all = jnp.concatenate([ctx_v.astype(v.dtype), v], axis=2)
    o = blocked_attention(q, k_all, v_all)
    return merge_heads(o) @ w_o, k, v


def swiglu(x, w1, w3, w2):
    return (jax.nn.silu(x @ w1) * (x @ w3)) @ w2


def moe_swiglu(x, w_router, w1, w3, w2):
    logits = (x @ w_router).astype(jnp.float32)
    top_v, top_i = lax.top_k(logits, TOP_K)
    gk = jax.nn.softmax(top_v, axis=-1)
    gates = jnp.sum(jax.nn.one_hot(top_i, N_EXPERTS, dtype=jnp.float32) * gk[..., None], axis=-2)
    out = jnp.zeros(x.shape, jnp.float32)
    for e in range(N_EXPERTS):
        out = out + gates[..., e:e + 1] * swiglu(x, w1[e], w3[e], w2[e]).astype(jnp.float32)
    return out.astype(x.dtype)


def setup_inputs(seed: int = 0) -> dict:
    key = jax.random.key(seed)
    ks = jax.random.split(key, 24)
    f32 = jnp.float32
    d = D_MODEL
    n_ret = (DEPTH + 1) // 2
    n_att = DEPTH // 2

    def nrm(k, shape, scale):
        return jax.random.normal(k, shape, f32) * scale

    base_logit = jnp.log(jnp.exp2(5.0 + jnp.arange(RET_HEADS, dtype=f32)) - 1.0)
    return {
        'x_prompt': nrm(ks[0], (BATCH, SEQ, d), 1.0),
        'x_sample': nrm(ks[1], (DEC_BATCH, DEC_SEQ, d), 1.0),
        'state_ret': nrm(ks[2], (DEC_BATCH, n_ret, 2, RET_HEADS, RET_DK, RET_DV), 0.5),
        'cache_k': nrm(ks[3], (DEC_BATCH, n_att, KV_HEADS, PAST_LEN, ATTN_HEAD_DIM), 1.0),
        'cache_v': nrm(ks[4], (DEC_BATCH, n_att, KV_HEADS, PAST_LEN, ATTN_HEAD_DIM), 1.0),
        'c': nrm(ks[5], (DEC_BATCH, d), 1.0),
        'c_ctx': nrm(ks[6], (d,), 1.0),
        'w_mod': nrm(ks[7], (DEPTH, d, N_MOD * d), 0.5 * d ** -0.5),
        'b_mod': nrm(ks[8], (DEPTH, N_MOD * d), 0.02),
        'norm_g': 1.0 + nrm(ks[9], (DEPTH, 4, d), 0.02),
        'ret_w_in': nrm(ks[10], (n_ret, d, RET_IN_DIM), d ** -0.5),
        'ret_decay_logit': base_logit + nrm(ks[11], (n_ret, 2, RET_HEADS), 0.1),
        'ret_w_out': nrm(ks[12], (n_ret, RET_V, d), RET_V ** -0.5),
        'attn_w_qkv': nrm(ks[13], (n_att, d, ATTN_QKV_DIM), d ** -0.5),
        'attn_q_norm': 1.0 + nrm(ks[14], (n_att, ATTN_HEAD_DIM), 0.02),
        'attn_k_norm': 1.0 + nrm(ks[15], (n_att, ATTN_HEAD_DIM), 0.02),
        'attn_w_o': nrm(ks[16], (n_att, ATTN_Q, d), ATTN_Q ** -0.5),
        'ffn_w1': nrm(ks[17], (n_ret, d, FF_DENSE), d ** -0.5),
        'ffn_w3': nrm(ks[18], (n_ret, d, FF_DENSE), d ** -0.5),
        'ffn_w2': nrm(ks[19], (n_ret, FF_DENSE, d), FF_DENSE ** -0.5),
        'moe_router': nrm(ks[20], (n_att, d, N_EXPERTS), d ** -0.5),
        'moe_w1': nrm(ks[21], (n_att, N_EXPERTS, d, FF_EXPERT), d ** -0.5),
        'moe_w3': nrm(ks[22], (n_att, N_EXPERTS, d, FF_EXPERT), d ** -0.5),
        'moe_w2': nrm(ks[23], (n_att, N_EXPERTS, FF_EXPERT, d), FF_EXPERT ** -0.5),
    }


def reference(x_prompt, x_sample, state_ret, cache_k, cache_v, c, c_ctx, w_mod, b_mod, norm_g,
              ret_w_in, ret_decay_logit, ret_w_out, attn_w_qkv, attn_q_norm, attn_k_norm, attn_w_o,
              ffn_w1, ffn_w3, ffn_w2, moe_router, moe_w1, moe_w3, moe_w2):
    xp, xs = x_prompt, x_sample
    bp = xp.shape[0]
    new_ret, new_k, new_v = [], [], []
    for i in range(DEPTH):
        j = i // 2
        sh1_p, sc1_p, gt1_p, sh2_p, sc2_p, gt2_p = modulation(c_ctx[None, :], w_mod[i], b_mod[i])
        sh1_s, sc1_s, gt1_s, sh2_s, sc2_s, gt2_s = modulation(c, w_mod[i], b_mod[i])
        hp = modulate(xp, norm_g[i, 0], sh1_p, sc1_p)
        hs = modulate(xs, norm_g[i, 0], sh1_s, sc1_s)
        if i % 2 == 0:
            zero_state = jnp.zeros((bp, 2, RET_HEADS, RET_DK, RET_DV), jnp.float32)
            mp, st = retention_mixer(hp, ret_w_in[j], ret_decay_logit[j], ret_w_out[j], zero_state, False)
            ms, _ = retention_mixer(hs, ret_w_in[j], ret_decay_logit[j], ret_w_out[j], state_ret[:, j], True)
            new_ret.append(st)
        else:
            mp, kp, vp = attention_mixer(hp, attn_w_qkv[j], attn_q_norm[j], attn_k_norm[j], attn_w_o[j], None, None)
            ms, _, _ = attention_mixer(hs, attn_w_qkv[j], attn_q_norm[j], attn_k_norm[j], attn_w_o[j],
                                       cache_k[:, j], cache_v[:, j])
            new_k.append(kp)
            new_v.append(vp)
        xp = xp + gt1_p * rms_norm(mp, norm_g[i, 1])
        xs = xs + gt1_s * rms_norm(ms, norm_g[i, 1])
        hp = modulate(xp, norm_g[i, 2], sh2_p, sc2_p)
        hs = modulate(xs, norm_g[i, 2], sh2_s, sc2_s)
        if i % 2 == 0:
            fp = swiglu(hp, ffn_w1[j], ffn_w3[j], ffn_w2[j])
            fs = swiglu(hs, ffn_w1[j], ffn_w3[j], ffn_w2[j])
        else:
            fp = moe_swiglu(hp, moe_router[j], moe_w1[j], moe_w3[j], moe_w2[j])
            fs = moe_swiglu(hs, moe_router[j], moe_w1[j], moe_w3[j], moe_w2[j])
        xp = xp + gt2_p * rms_norm(fp, norm_g[i, 3])
        xs = xs + gt2_s * rms_norm(fs, norm_g[i, 3])
    new_state_ret = jnp.stack(new_ret, axis=1)
    new_cache_k = jnp.stack(new_k, axis=1)
    new_cache_v = jnp.stack(new_v, axis=1)
    return (xp, xs, new_state_ret, new_cache_k, new_cache_v)
```

```python
import functools

import jax
import jax.numpy as jnp
from jax import lax
from jax.experimental import pallas as pl
from jax.experimental.pallas import tpu as pltpu

BLOCK = 128
GRID_W = 64
RMS_EPS = 1e-6
ROPE_THETA = 10000.0
N_MOD = 6
RET_HEADS = 16
ATTN_HEADS = 32
KV_HEADS = 8
TOP_K = 2

V7X_VMEM_BYTES = 64 * 1024 * 1024
V7X_VMEM_LIMIT = V7X_VMEM_BYTES * 7 // 8
LANES = 128
SUBLANES = 8

MM_TM = 1024
MM_TN = 1024
MM_TK = 4096
FFN_TN = 512
FFN_PAD = 1024
ROW_TB = 128
QKV_TB = 256
RET_TT = 1024
ATTN_TQ = 128
ROUTER_TB = 256
MOE_TM = 1024
MOE_SUB = 256
MOE_TN = 512
MOE_TK = 3584

_BF16 = jnp.bfloat16
_F32 = jnp.float32


def _tile(dim, target, quantum):
    best = None
    t = quantum
    while t <= min(dim, target):
        if dim % t == 0:
            best = t
        t += quantum
    return dim if best is None else best


def _params(*sem):
    return pltpu.CompilerParams(dimension_semantics=sem, vmem_limit_bytes=V7X_VMEM_LIMIT)


def _dot(a, b):
    return jnp.dot(a, b, preferred_element_type=_F32)


def _silu(x):
    return x * jax.nn.sigmoid(x)


def _rms(x, g):
    return x * lax.rsqrt(jnp.mean(x * x, axis=-1, keepdims=True) + RMS_EPS) * g


def _modulation_kernel(c_ref, w_ref, b_ref, o_ref):
    s = _silu(c_ref[...]).astype(_BF16)
    o_ref[...] = _dot(s, w_ref[...].astype(_BF16)) + b_ref[...]


def _modulation(cond, w_mod, b_mod):
    depth, d, nd = w_mod.shape
    g = cond.shape[0]
    gp = -(-g // SUBLANES) * SUBLANES
    cond_p = jnp.zeros((gp, d), _F32).at[:g].set(cond)
    tn = _tile(nd, 512, LANES)
    out = pl.pallas_call(
        _modulation_kernel,
        grid=(depth, nd // tn),
        in_specs=[
            pl.BlockSpec((gp, d), lambda l, j: (0, 0)),
            pl.BlockSpec((None, d, tn), lambda l, j: (l, 0, j)),
            pl.BlockSpec((None, 1, tn), lambda l, j: (l, 0, j)),
        ],
        out_specs=pl.BlockSpec((None, gp, tn), lambda l, j: (l, 0, j)),
        out_shape=jax.ShapeDtypeStruct((depth, gp, nd), _F32),
        compiler_params=_params("parallel", "parallel"),
        name="modulation",
    )(cond_p, w_mod, b_mod.reshape(depth, 1, nd))
    return out[:, :g].reshape(depth, g, N_MOD, d)


def _post_kernel(*refs, has_update, gate_idx, has_next, shift_idx, scale_idx, want_f32):
    it = iter(refs)
    x_ref = next(it)
    if has_update:
        m_ref, gpost_ref, gmod_ref = next(it), next(it), next(it)
    if has_next:
        gnext_ref, nmod_ref = next(it), next(it)
    x = x_ref[...]
    if has_update:
        xo_ref = next(it)
        x = x + gmod_ref[0, pl.ds(gate_idx, 1), :] * _rms(m_ref[...], gpost_ref[...])
        xo_ref[...] = x
    if has_next:
        h = (_rms(x, gnext_ref[...]) * (1.0 + nmod_ref[0, pl.ds(scale_idx, 1), :])
             + nmod_ref[0, pl.ds(shift_idx, 1), :])
        h_ref = next(it)
        h_ref[...] = h.astype(h_ref.dtype)
        if want_f32:
            h32_ref = next(it)
            h32_ref[...] = h


def _post(x, group_of, *, update=None, nxt=None, want_f32=False):
    n, d = x.shape
    tb = ROW_TB
    row = pl.BlockSpec((tb, d), lambda i: (i, 0))
    vec = pl.BlockSpec((1, d), lambda i: (0, 0))

    def mod_spec(mods):
        return pl.BlockSpec((1, N_MOD, d), lambda i: (group_of(i * tb), 0, 0))

    args, in_specs, out_specs, out_shape = [x], [row], [], []
    gate_idx = shift_idx = scale_idx = 0
    if update is not None:
        m, g_post, mods, gate_idx = update
        args += [m, g_post.reshape(1, d), mods]
        in_specs += [row, vec, mod_spec(mods)]
        out_specs.append(row)
        out_shape.append(jax.ShapeDtypeStruct((n, d), _F32))
    if nxt is not None:
        g_next, nmods, shift_idx, scale_idx = nxt
        args += [g_next.reshape(1, d), nmods]
        in_specs += [vec, mod_spec(nmods)]
        out_specs.append(row)
        out_shape.append(jax.ShapeDtypeStruct((n, d), _BF16))
        if want_f32:
            out_specs.append(row)
            out_shape.append(jax.ShapeDtypeStruct((n, d), _F32))
    return pl.pallas_call(
        functools.partial(_post_kernel, has_update=update is not None, gate_idx=gate_idx,
                          has_next=nxt is not None, shift_idx=shift_idx, scale_idx=scale_idx,
                          want_f32=want_f32),
        grid=(n // tb,),
        in_specs=in_specs,
        out_specs=out_specs,
        out_shape=out_shape,
        compiler_params=_params("parallel"),
        name="post_norm_modulate",
    )(*args)


def _mm_kernel(x_ref, w_ref, o_ref):
    o_ref[...] = _dot(x_ref[...], w_ref[...]).astype(o_ref.dtype)


def _mm_acc_kernel(x_ref, w_ref, o_ref):
    @pl.when(pl.program_id(2) == 0)
    def _():
        o_ref[...] = jnp.zeros_like(o_ref)

    o_ref[...] += _dot(x_ref[...], w_ref[...])


def _matmul(x, w, out_dtype, *, tk_target=MM_TK, name="matmul"):
    m, k = x.shape
    n = w.shape[1]
    tm, tn, tk = _tile(m, MM_TM, SUBLANES), _tile(n, MM_TN, LANES), _tile(k, tk_target, LANES)
    if tk == k:
        return pl.pallas_call(
            _mm_kernel,
            grid=(m // tm, n // tn),
            in_specs=[pl.BlockSpec((tm, k), lambda i, j: (i, 0)),
                      pl.BlockSpec((k, tn), lambda i, j: (0, j))],
            out_specs=pl.BlockSpec((tm, tn), lambda i, j: (i, j)),
            out_shape=jax.ShapeDtypeStruct((m, n), out_dtype),
            compiler_params=_params("parallel", "parallel"),
            name=name,
        )(x, w)
    assert out_dtype == _F32, "the K-tiled matmul accumulates in its f32 output block"
    return pl.pallas_call(
        _mm_acc_kernel,
        grid=(m // tm, n // tn, k // tk),
        in_specs=[pl.BlockSpec((tm, tk), lambda i, j, kk: (i, kk)),
                  pl.BlockSpec((tk, tn), lambda i, j, kk: (kk, j))],
        out_specs=pl.BlockSpec((tm, tn), lambda i, j, kk: (i, j)),
        out_shape=jax.ShapeDtypeStruct((m, n), _F32),
        compiler_params=_params("parallel", "parallel", "arbitrary"),
        name=name,
    )(x, w)


def _swiglu_kernel(x_ref, w1_ref, w3_ref, o_ref):
    x = x_ref[...]
    o_ref[...] = (_silu(_dot(x, w1_ref[...])) * _dot(x, w3_ref[...])).astype(o_ref.dtype)


def _swiglu_up(x, w1, w3):
    m, k = x.shape
    f = w1.shape[1]
    tm, tn = _tile(m, MM_TM, SUBLANES), _tile(f, FFN_TN, LANES)
    return pl.pallas_call(
        _swiglu_kernel,
        grid=(m // tm, f // tn),
        in_specs=[pl.BlockSpec((tm, k), lambda i, j: (i, 0)),
                  pl.BlockSpec((k, tn), lambda i, j: (0, j)),
                  pl.BlockSpec((k, tn), lambda i, j: (0, j))],
        out_specs=pl.BlockSpec((tm, tn), lambda i, j: (i, j)),
        out_shape=jax.ShapeDtypeStruct((m, f), _BF16),
        compiler_params=_params("parallel", "parallel"),
        name="ffn_swiglu_up",
    )(x, w1, w3)


def _rope_tables(t, d):
    rows = t // GRID_W
    n_freq = d // 4
    inv = ROPE_THETA ** (-jnp.arange(n_freq, dtype=_F32) / n_freq)
    row = jnp.repeat(jnp.arange(rows, dtype=_F32), GRID_W)
    col = jnp.tile(jnp.arange(GRID_W, dtype=_F32), rows)
    ang = jnp.concatenate([row[:, None] * inv, col[:, None] * inv], axis=-1)
    return jnp.cos(ang), jnp.sin(ang)


def _retention_kernel(lg_ref, *refs, backward, rope, has_init, has_prev, has_final, n_heads, dk):
    it = iter(refs)
    q_ref, k_ref, v_ref, g_ref = next(it), next(it), next(it), next(it)
    if rope:
        cos_ref, sin_ref = next(it), next(it)
    if has_init:
        s0_ref = next(it)
    if has_prev:
        prev_ref = next(it)
    y_ref = next(it)
    if has_final:
        fin_ref = next(it)
    s_ref = next(it)

    h = pl.program_id(1)
    tt = pl.program_id(2)
    L = BLOCK
    n_chunks = q_ref.shape[0] // L
    half = dk // 2

    @pl.when(tt == 0)
    def _():
        s_ref[...] = s0_ref[...] if has_init else jnp.zeros_like(s_ref)

    lg = lg_ref[(n_heads if backward else 0) + h]
    r = lax.broadcasted_iota(jnp.int32, (L, L), 0)
    c = lax.broadcasted_iota(jnp.int32, (L, L), 1)
    diff = ((c - r) if backward else (r - c)).astype(_F32)
    intra = jnp.where(diff >= 0, jnp.exp(lg * jnp.maximum(diff, 0.0)), 0.0)
    idx = lax.broadcasted_iota(jnp.int32, (L, 1), 0).astype(_F32)
    pos = (L - 1.0 - idx) if backward else idx
    cross = jnp.exp(lg * (pos + 1.0))
    tail = jnp.exp(lg * (L - 1.0 - pos))
    decay = jnp.exp(jnp.full((1, 1), lg * L, _F32))

    def rotate(x, cos, sin):
        x1, x2 = x[:, :half], x[:, half:]
        return jnp.concatenate([x1 * cos - x2 * sin, x2 * cos + x1 * sin], axis=-1)

    def body(ci, carry):
        cc = (n_chunks - 1 - ci) if backward else ci
        rows = pl.ds(pl.multiple_of(cc * L, L), L)
        q = q_ref[rows, :]
        k = k_ref[rows, :]
        if rope:
            cos, sin = cos_ref[rows, :], sin_ref[rows, :]
            q, k = rotate(q, cos, sin), rotate(k, cos, sin)
        k = k * (dk ** -0.5)
        qb = q.astype(_BF16)
        vb = v_ref[rows, :].astype(_BF16)
        scores = lax.dot_general(qb, k.astype(_BF16), (((1,), (1,)), ((), ())),
                                 preferred_element_type=_F32) * intra
        s = s_ref[...]
        o = _dot(scores.astype(_BF16), vb) + _dot(qb, s.astype(_BF16)) * cross
        kt = (k * tail).T.astype(_BF16)
        s_ref[...] = s * decay + _dot(kt, vb)
        oc = o - jnp.mean(o, axis=-1, keepdims=True)
        on = oc * lax.rsqrt(jnp.mean(oc * oc, axis=-1, keepdims=True) + RMS_EPS)
        y = _silu(g_ref[rows, :]) * on
        if has_prev:
            y = prev_ref[rows, :] + y
        y_ref[rows, :] = y.astype(y_ref.dtype)
        return carry

    lax.fori_loop(0, n_chunks, body, 0)

    if has_final:
        @pl.when(tt == pl.num_programs(2) - 1)
        def _():
            fin_ref[...] = s_ref[...]


def _retention_direction(proj, log_gamma, *, row0, n_seq, t, backward, rope_tabs, init_state,
                         prev, out_dtype, want_final):
    nh = RET_HEADS
    width = proj.shape[1]
    dk = width // (8 * nh)
    dv = 2 * dk
    qk_w, v_w = nh * dk, nh * dv
    tt = _tile(t, RET_TT, BLOCK)
    ntt = t // tt
    rb0 = row0 // tt

    def tile_of(ti):
        return (ntt - 1 - ti) if backward else ti

    def rows(b, ti):
        return rb0 + b * ntt + tile_of(ti)

    gate_col = (2 * qk_w + v_w + (v_w if backward else 0)) // dv
    in_specs = [
        pl.BlockSpec((tt, dk), lambda b, h, ti, lg: (rows(b, ti), h)),
        pl.BlockSpec((tt, dk), lambda b, h, ti, lg: (rows(b, ti), qk_w // dk + h)),
        pl.BlockSpec((tt, dv), lambda b, h, ti, lg: (rows(b, ti), 2 * qk_w // dv + h)),
        pl.BlockSpec((tt, dv), lambda b, h, ti, lg: (rows(b, ti), gate_col + h)),
    ]
    args = [proj, proj, proj, proj]
    if rope_tabs is not None:
        spec = pl.BlockSpec((tt, dk // 2), lambda b, h, ti, lg: (tile_of(ti), 0))
        in_specs += [spec, spec]
        args += list(rope_tabs)
    if init_state is not None:
        in_specs.append(pl.BlockSpec((None, None, None, dk, dv),
                                     lambda b, h, ti, lg: (b, 1 if backward else 0, h, 0, 0)))
        args.append(init_state)
    y_spec = pl.BlockSpec((tt, dv), lambda b, h, ti, lg: (b * ntt + tile_of(ti), h))
    if prev is not None:
        in_specs.append(y_spec)
        args.append(prev)
    out_specs = [y_spec]
    out_shape = [jax.ShapeDtypeStruct((n_seq * t, v_w), out_dtype)]
    if want_final:
        out_specs.append(pl.BlockSpec((None, None, dk, dv), lambda b, h, ti, lg: (b, h, 0, 0)))
        out_shape.append(jax.ShapeDtypeStruct((n_seq, nh, dk, dv), _F32))
    res = pl.pallas_call(
        functools.partial(_retention_kernel, backward=backward, rope=rope_tabs is not None,
                          has_init=init_state is not None, has_prev=prev is not None,
                          has_final=want_final, n_heads=nh, dk=dk),
        grid_spec=pltpu.PrefetchScalarGridSpec(
            num_scalar_prefetch=1,
            grid=(n_seq, nh, ntt),
            in_specs=in_specs,
            out_specs=out_specs,
            scratch_shapes=[pltpu.VMEM((dk, dv), _F32)],
        ),
        out_shape=out_shape,
        compiler_params=_params("parallel", "parallel", "arbitrary"),
        name="retention_bwd" if backward else "retention_fwd",
    )(log_gamma.reshape(-1), *args)
    return res if want_final else (res[0], None)


def _retention(proj, decay_logit, *, row0, n_seq, t, rope_tabs, init_state, want_final):
    log_gamma = jax.nn.log_sigmoid(decay_logit.astype(_F32))
    common = dict(row0=row0, n_seq=n_seq, t=t, rope_tabs=rope_tabs, init_state=init_state,
                  want_final=want_final)
    y_f, s_f = _retention_direction(proj, log_gamma, backward=False, prev=None,
                                    out_dtype=_F32, **common)
    y, s_b = _retention_direction(proj, log_gamma, backward=True, prev=y_f,
                                  out_dtype=_BF16, **common)
    return y, (jnp.stack([s_f, s_b], axis=1) if want_final else None)


def _qkv_post_kernel(*refs, n_q, n_kv, hd, rope, want_cache):
    it = iter(refs)
    x_ref, qn_ref, kn_ref = next(it), next(it), next(it)
    if rope:
        cos_ref, sin_ref = next(it), next(it)
    q_ref, k_ref, v_ref = next(it), next(it), next(it)
    if want_cache:
        ck_ref, cv_ref = next(it), next(it)

    def head(col, g_ref):
        x = x_ref[:, col * hd:(col + 1) * hd]
        return x * lax.rsqrt(jnp.mean(x * x, axis=-1, keepdims=True) + RMS_EPS) * g_ref[...]

    def rotate(x):
        if not rope:
            return x
        return x * cos_ref[...] + pltpu.roll(x, hd // 2, 1) * sin_ref[...]

    for hh in range(n_q):
        q_ref[:, hh * hd:(hh + 1) * hd] = rotate(head(hh, qn_ref)).astype(q_ref.dtype)
    for hh in range(n_kv):
        kn = head(n_q + hh, kn_ref)
        v = x_ref[:, (n_q + n_kv + hh) * hd:(n_q + n_kv + hh + 1) * hd]
        if want_cache:
            ck_ref[0, hh] = kn
            cv_ref[0, hh] = v
        k_ref[0, hh] = rotate(kn).astype(k_ref.dtype)
        v_ref[0, hh] = v.astype(v_ref.dtype)


def _qkv_post(qkv, q_norm, k_norm, *, row0, n_seq, t, rope_tabs, want_cache):
    n_q, n_kv = ATTN_HEADS, KV_HEADS
    hd = qkv.shape[1] // (n_q + 2 * n_kv)
    tb = _tile(t, QKV_TB, SUBLANES)
    nt = t // tb
    rb0 = row0 // tb
    in_specs = [pl.BlockSpec((tb, qkv.shape[1]), lambda i: (rb0 + i, 0)),
                pl.BlockSpec((1, hd), lambda i: (0, 0)),
                pl.BlockSpec((1, hd), lambda i: (0, 0))]
    args = [qkv, q_norm.reshape(1, hd), k_norm.reshape(1, hd)]
    if rope_tabs is not None:
        spec = pl.BlockSpec((tb, hd), lambda i: (i % nt, 0))
        in_specs += [spec, spec]
        args += list(rope_tabs)
    head_spec = pl.BlockSpec((1, n_kv, tb, hd), lambda i: (i // nt, 0, i % nt, 0))
    out_specs = [pl.BlockSpec((tb, n_q * hd), lambda i: (i, 0)), head_spec, head_spec]
    out_shape = [jax.ShapeDtypeStruct((n_seq * t, n_q * hd), _BF16),
                 jax.ShapeDtypeStruct((n_seq, n_kv, t, hd), _BF16),
                 jax.ShapeDtypeStruct((n_seq, n_kv, t, hd), _BF16)]
    if want_cache:
        out_specs += [head_spec, head_spec]
        out_shape += [jax.ShapeDtypeStruct((n_seq, n_kv, t, hd), _F32)] * 2
    return pl.pallas_call(
        functools.partial(_qkv_post_kernel, n_q=n_q, n_kv=n_kv, hd=hd,
                          rope=rope_tabs is not None, want_cache=want_cache),
        grid=(n_seq * nt,),
        in_specs=in_specs,
        out_specs=out_specs,
        out_shape=out_shape,
        compiler_params=_params("parallel"),
        name="qkv_norm_rope",
    )(*args)


def _attention_kernel(q_ref, k_ref, v_ref, o_ref, *, group, hd):
    tq = q_ref.shape[0]
    q = jnp.concatenate([q_ref[:, g * hd:(g + 1) * hd] for g in range(group)], axis=0)
    s = lax.dot_general(q, k_ref[...], (((1,), (1,)), ((), ())), preferred_element_type=_F32)
    e = jnp.exp((s - jnp.max(s, axis=-1, keepdims=True)) * (hd ** -0.5))
    inv = 1.0 / jnp.sum(e, axis=-1, keepdims=True)
    o = _dot(e.astype(_BF16), v_ref[...]) * inv
    for g in range(group):
        o_ref[:, g * hd:(g + 1) * hd] = o[g * tq:(g + 1) * tq].astype(o_ref.dtype)


def _attention(q, k, v, *, n_seq, t):
    n_kv, n_keys, hd = k.shape[1], k.shape[2], k.shape[3]
    group = ATTN_HEADS // n_kv
    tq = _tile(t, ATTN_TQ if n_keys > 1024 else 256, SUBLANES)
    nt = t // tq
    q_spec = pl.BlockSpec((tq, group * hd), lambda b, kh, i: (b * nt + i, kh))
    kv_spec = pl.BlockSpec((None, None, n_keys, hd), lambda b, kh, i: (b, kh, 0, 0))
    return pl.pallas_call(
        functools.partial(_attention_kernel, group=group, hd=hd),
        grid=(n_seq, n_kv, nt),
        in_specs=[q_spec, kv_spec, kv_spec],
        out_specs=q_spec,
        out_shape=jax.ShapeDtypeStruct(q.shape, _BF16),
        compiler_params=_params("parallel", "parallel", "parallel"),
        name="attention",
    )(q, k, v)


def _router_kernel(h_ref, w_ref, mi_ref, mf_ref, cnt_ref, run_ref, *, n_exp):
    @pl.when(pl.program_id(0) == 0)
    def _():
        run_ref[...] = jnp.zeros_like(run_ref)

    tb = h_ref.shape[0]
    logits = jnp.dot(h_ref[...], w_ref[...], precision=lax.Precision.HIGHEST,
                     preferred_element_type=_F32)
    lane = lax.broadcasted_iota(jnp.int32, logits.shape, 1).astype(_F32)
    neg = -jnp.inf
    l1 = jnp.where(lane < n_exp, logits, neg)
    m1 = jnp.max(l1, axis=-1, keepdims=True)
    i1 = jnp.min(jnp.where(l1 == m1, lane, float(LANES)), axis=-1, keepdims=True)
    l2 = jnp.where(lane == i1, neg, l1)
    m2 = jnp.max(l2, axis=-1, keepdims=True)
    i2 = jnp.min(jnp.where(l2 == m2, lane, float(LANES)), axis=-1, keepdims=True)
    e2 = jnp.exp(m2 - m1)
    g1 = 1.0 / (1.0 + e2)
    g2 = e2 / (1.0 + e2)
    oh1 = (lane == i1).astype(_F32)
    oh2 = (lane == i2).astype(_F32)
    sel = oh1 + oh2
    r = lax.broadcasted_iota(jnp.int32, (tb, tb), 0)
    c = lax.broadcasted_iota(jnp.int32, (tb, tb), 1)
    before = _dot((c < r).astype(_BF16), sel.astype(_BF16)) + run_ref[...]
    rank1 = jnp.sum(before * oh1, axis=-1, keepdims=True)
    rank2 = jnp.sum(before * oh2, axis=-1, keepdims=True)
    run_ref[...] += jnp.sum(sel, axis=0, keepdims=True)
    cnt_ref[...] = run_ref[...].astype(jnp.int32)
    mi = jnp.where(lane == 0, i1, jnp.where(lane == 1, i2,
                   jnp.where(lane == 2, rank1, jnp.where(lane == 3, rank2, 0.0))))
    mi_ref[...] = mi.astype(jnp.int32)
    mf_ref[...] = jnp.where(lane == 0, g1, jnp.where(lane == 1, g2, 0.0))


def _router(h32, w_router):
    n, d = h32.shape
    n_exp = w_router.shape[1]
    w = jnp.zeros((d, LANES), _F32).at[:, :n_exp].set(w_router)
    tb = _tile(n, ROUTER_TB, SUBLANES)
    row = pl.BlockSpec((tb, LANES), lambda i: (i, 0))
    return pl.pallas_call(
        functools.partial(_router_kernel, n_exp=n_exp),
        grid=(n // tb,),
        in_specs=[pl.BlockSpec((tb, d), lambda i: (i, 0)), pl.BlockSpec((d, LANES), lambda i: (0, 0))],
        out_specs=[row, row, pl.BlockSpec((1, LANES), lambda i: (0, 0))],
        out_shape=[jax.ShapeDtypeStruct((n, LANES), jnp.int32),
                   jax.ShapeDtypeStruct((n, LANES), _F32),
                   jax.ShapeDtypeStruct((1, LANES), jnp.int32)],
        scratch_shapes=[pltpu.VMEM((1, LANES), _F32)],
        compiler_params=_params("arbitrary"),
        name="moe_router",
    )(h32, w)


def _copy_row_kernel(pos_ref, h_ref, z_ref, o_ref):
    del pos_ref, z_ref
    o_ref[...] = h_ref[...]


def _dispatch(h32, pos, n_rows):
    n, d = h32.shape
    zeros = jnp.zeros((n_rows, 1, d), _F32)
    out = pl.pallas_call(
        _copy_row_kernel,
        grid_spec=pltpu.PrefetchScalarGridSpec(
            num_scalar_prefetch=1,
            grid=(pos.shape[0],),
            in_specs=[pl.BlockSpec((1, 1, d), lambda r, pos: (r % n, 0, 0)),
                      pl.BlockSpec(memory_space=pl.ANY)],
            out_specs=pl.BlockSpec((1, 1, d), lambda r, pos: (pos[r], 0, 0)),
        ),
        out_shape=jax.ShapeDtypeStruct((n_rows, 1, d), _F32),
        input_output_aliases={2: 0},
        compiler_params=_params("arbitrary"),
        name="moe_dispatch",
    )(pos, h32.reshape(n, 1, d), zeros)
    return out.reshape(n_rows, d)


def _group_swiglu_kernel(te_ref, ns_ref, x_ref, w1_ref, w3_ref, o_ref, *, sub):
    del te_ref
    n_sub = ns_ref[pl.program_id(0)]
    full = x_ref.shape[0] // sub

    def act(x):
        return (_silu(_dot(x, w1_ref[...])) * _dot(x, w3_ref[...])).astype(o_ref.dtype)

    @pl.when(n_sub == full)
    def _():
        o_ref[...] = act(x_ref[...])

    @pl.when(n_sub < full)
    def _():
        o_ref[...] = jnp.zeros_like(o_ref)

        def body(s, carry):
            rows = pl.ds(pl.multiple_of(s * sub, sub), sub)
            o_ref[rows, :] = act(x_ref[rows, :])
            return carry

        lax.fori_loop(0, n_sub, body, 0)


def _group_down_kernel(te_ref, ns_ref, x_ref, w_ref, o_ref, *, sub):
    del te_ref
    n_sub = ns_ref[pl.program_id(0)]
    full = x_ref.shape[0] // sub

    @pl.when(pl.program_id(2) == 0)
    def _():
        o_ref[...] = jnp.zeros_like(o_ref)

    @pl.when(n_sub == full)
    def _():
        o_ref[...] += _dot(x_ref[...], w_ref[...])

    @pl.when(n_sub < full)
    def _():
        def body(s, carry):
            rows = pl.ds(pl.multiple_of(s * sub, sub), sub)
            o_ref[rows, :] += _dot(x_ref[rows, :], w_ref[...])
            return carry

        lax.fori_loop(0, n_sub, body, 0)


def _grouped_experts(xg, tile_expert, tile_nsub, w1, w3, w2, *, tm, sub):
    r, d = xg.shape
    f = w1.shape[2]
    nt = r // tm
    tn = _tile(f, MOE_TN, LANES)
    a = pl.pallas_call(
        functools.partial(_group_swiglu_kernel, sub=sub),
        grid_spec=pltpu.PrefetchScalarGridSpec(
            num_scalar_prefetch=2,
            grid=(nt, f // tn),
            in_specs=[pl.BlockSpec((tm, d), lambda i, j, te, ns: (i, 0)),
                      pl.BlockSpec((None, d, tn), lambda i, j, te, ns: (te[i], 0, j)),
                      pl.BlockSpec((None, d, tn), lambda i, j, te, ns: (te[i], 0, j))],
            out_specs=pl.BlockSpec((tm, tn), lambda i, j, te, ns: (i, j)),
        ),
        out_shape=jax.ShapeDtypeStruct((r, f), _BF16),
        compiler_params=_params("parallel", "parallel"),
        name="moe_swiglu_up",
    )(tile_expert, tile_nsub, xg, w1, w3)
    tn2, tk = _tile(d, MM_TN, LANES), _tile(f, MOE_TK, LANES)
    return pl.pallas_call(
        functools.partial(_group_down_kernel, sub=sub),
        grid_spec=pltpu.PrefetchScalarGridSpec(
            num_scalar_prefetch=2,
            grid=(nt, d // tn2, f // tk),
            in_specs=[pl.BlockSpec((tm, tk), lambda i, j, k, te, ns: (i, k)),
                      pl.BlockSpec((None, tk, tn2), lambda i, j, k, te, ns: (te[i], k, j))],
            out_specs=pl.BlockSpec((tm, tn2), lambda i, j, k, te, ns: (i, j)),
        ),
        out_shape=jax.ShapeDtypeStruct((r, d), _F32),
        compiler_params=_params("parallel", "parallel", "arbitrary"),
        name="moe_down",
    )(tile_expert, tile_nsub, a, w2)


def _combine_kernel(p1_ref, p2_ref, y1_ref, y2_ref, gate_ref, x_ref, g_ref, mod_ref, o_ref, *, gate_idx):
    del p1_ref, p2_ref
    gates = gate_ref[0]
    f = gates[:, 0:1] * y1_ref[0] + gates[:, 1:2] * y2_ref[0]
    o_ref[0] = x_ref[0] + mod_ref[0, pl.ds(gate_idx, 1), :] * _rms(f, g_ref[...])


def _combine(y, pos1, pos2, gates, x, g_post, mods, gate_idx, group_of):
    n, d = x.shape
    r = y.shape[0]
    row = pl.BlockSpec((1, 1, d), lambda t, p1, p2: (t, 0, 0))
    out = pl.pallas_call(
        functools.partial(_combine_kernel, gate_idx=gate_idx),
        grid_spec=pltpu.PrefetchScalarGridSpec(
            num_scalar_prefetch=2,
            grid=(n,),
            in_specs=[pl.BlockSpec((1, 1, d), lambda t, p1, p2: (p1[t], 0, 0)),
                      pl.BlockSpec((1, 1, d), lambda t, p1, p2: (p2[t], 0, 0)),
                      pl.BlockSpec((1, 1, LANES), lambda t, p1, p2: (t, 0, 0)),
                      row,
                      pl.BlockSpec((1, d), lambda t, p1, p2: (0, 0)),
                      pl.BlockSpec((1, N_MOD, d), lambda t, p1, p2: (group_of(t), 0, 0))],
            out_specs=row,
        ),
        out_shape=jax.ShapeDtypeStruct((n, 1, d), _F32),
        compiler_params=_params("parallel"),
        name="moe_combine",
    )(pos1, pos2, y.reshape(r, 1, d), y.reshape(r, 1, d), gates.reshape(n, 1, LANES),
      x.reshape(n, 1, d), g_post.reshape(1, d), mods)
    return out.reshape(n, d)


def _moe(h_bf16, h32, x, w_router, w1, w3, w2, g_post, mods, gate_idx, group_of):
    del h_bf16
    n, d = h32.shape
    n_exp = w_router.shape[1]
    tm = _tile(TOP_K * n, MOE_TM, SUBLANES)
    sub = _tile(tm, MOE_SUB, SUBLANES)
    meta_i, meta_f, counts = _router(h32, w_router)
    cnt = counts[0, :n_exp]
    padded = (cnt + tm - 1) // tm * tm
    ends = jnp.cumsum(padded)
    offs = ends - padded
    pos1 = offs[meta_i[:, 0]] + meta_i[:, 2]
    pos2 = offs[meta_i[:, 1]] + meta_i[:, 3]
    n_rows = TOP_K * n + n_exp * tm
    n_tiles = n_rows // tm
    start = jnp.arange(n_tiles, dtype=jnp.int32) * tm
    tile_expert = jnp.minimum(jnp.searchsorted(ends, start, side="right"), n_exp - 1).astype(jnp.int32)
    live = cnt[tile_expert] - (start - offs[tile_expert])
    tile_nsub = jnp.clip((live + sub - 1) // sub, 0, tm // sub).astype(jnp.int32)
    xg = _dispatch(h32, jnp.concatenate([pos1, pos2]).astype(jnp.int32), n_rows).astype(_BF16)
    y = _grouped_experts(xg, tile_expert, tile_nsub, w1, w3, w2, tm=tm, sub=sub)
    return _combine(y, pos1.astype(jnp.int32), pos2.astype(jnp.int32), meta_f, x, g_post, mods,
                    gate_idx, group_of)


def kernel(x_prompt, x_sample, state_ret, cache_k, cache_v, c, c_ctx, w_mod, b_mod, norm_g,
           ret_w_in, ret_decay_logit, ret_w_out, attn_w_qkv, attn_q_norm, attn_k_norm, attn_w_o,
           ffn_w1, ffn_w3, ffn_w2, moe_router, moe_w1, moe_w3, moe_w2):
    bp, sp, d = x_prompt.shape
    bs, ss, _ = x_sample.shape
    n_p, n_s = bp * sp, bs * ss
    depth = w_mod.shape[0]

    def group_of(row):
        return jnp.where(row < n_p, 0, 1 + (row - n_p) // ss)

    x = jnp.concatenate([x_prompt.reshape(n_p, d), x_sample.reshape(n_s, d)], axis=0)
    mods = _modulation(jnp.concatenate([c_ctx[None, :], c], axis=0), w_mod, b_mod)

    new_ret, new_k, new_v = [], [], []
    (h,) = _post(x, group_of, nxt=(norm_g[0, 0], mods[0], 0, 1))
    for i in range(depth):
        j = i // 2
        md = mods[i]
        last = i == depth - 1
        if i % 2 == 0:
            proj = _matmul(h, ret_w_in[j].astype(_BF16), _F32, name="ret_in_proj")
            dk = proj.shape[1] // (8 * RET_HEADS)
            y_p, st = _retention(proj, ret_decay_logit[j], row0=0, n_seq=bp, t=sp, rope_tabs=None,
                                 init_state=None, want_final=True)
            y_s, _ = _retention(proj, ret_decay_logit[j], row0=n_p, n_seq=bs, t=ss,
                                rope_tabs=_rope_tables(ss, dk), init_state=state_ret[:, j],
                                want_final=False)
            new_ret.append(st)
            mix = _matmul(jnp.concatenate([y_p, y_s], axis=0), ret_w_out[j].astype(_BF16), _F32,
                          name="ret_out_proj")
        else:
            qkv = _matmul(h, attn_w_qkv[j].astype(_BF16), _F32, name="attn_qkv_proj")
            hd = qkv.shape[1] // (ATTN_HEADS + 2 * KV_HEADS)
            q_p, k_p, v_p, ck, cv = _qkv_post(qkv, attn_q_norm[j], attn_k_norm[j], row0=0, n_seq=bp,
                                              t=sp, rope_tabs=None, want_cache=True)
            cos, sin = _rope_tables(ss, hd)
            tabs = (jnp.concatenate([cos, cos], axis=-1), jnp.concatenate([-sin, sin], axis=-1))
            q_s, k_s, v_s = _qkv_post(qkv, attn_q_norm[j], attn_k_norm[j], row0=n_p, n_seq=bs, t=ss,
                                      rope_tabs=tabs, want_cache=False)
            new_k.append(ck)
            new_v.append(cv)
            o_p = _attention(q_p, k_p, v_p, n_seq=bp, t=sp)
            o_s = _attention(q_s,
                             jnp.concatenate([cache_k[:, j].astype(_BF16), k_s], axis=2),
                             jnp.concatenate([cache_v[:, j].astype(_BF16), v_s], axis=2),
                             n_seq=bs, t=ss)
            mix = _matmul(jnp.concatenate([o_p, o_s], axis=0), attn_w_o[j].astype(_BF16), _F32,
                          name="attn_out_proj")
        moe_layer = i % 2 == 1
        res = _post(x, group_of, update=(mix, norm_g[i, 1], md, 2), nxt=(norm_g[i, 2], md, 3, 4),
                    want_f32=moe_layer)
        x, h = res[0], res[1]
        if not moe_layer:
            f = ffn_w1.shape[2]
            fp = -(-f // FFN_PAD) * FFN_PAD
            pad_c = lambda w: jnp.pad(w.astype(_BF16), ((0, 0), (0, fp - f)))
            up = _swiglu_up(h, pad_c(ffn_w1[j]), pad_c(ffn_w3[j]))
            w2 = jnp.pad(ffn_w2[j].astype(_BF16), ((0, fp - f), (0, 0)))
            ff = _matmul(up, w2, _F32, tk_target=fp // 4, name="ffn_down")
            nxt = None if last else (norm_g[i + 1, 0], mods[i + 1], 0, 1)
            res = _post(x, group_of, update=(ff, norm_g[i, 3], md, 5), nxt=nxt)
            x = res[0]
            h = None if last else res[1]
        else:
            x = _moe(h, res[2], x, moe_router[j], moe_w1[j].astype(_BF16), moe_w3[j].astype(_BF16),
                     moe_w2[j].astype(_BF16), norm_g[i, 3], md, 5, group_of)
            if not last:
                (h,) = _post(x, group_of, nxt=(norm_g[i + 1, 0], mods[i + 1], 0, 1))

    return (x[:n_p].reshape(bp, sp, d), x[n_p:].reshape(bs, ss, d),
            jnp.stack(new_ret, axis=1), jnp.stack(new_k, axis=1), jnp.stack(new_v, axis=1))
```

```python
import functools

import jax
import jax.numpy as jnp
from jax import lax
from jax.experimental import pallas as pl
from jax.experimental.pallas import tpu as pltpu

BLOCK = 128
GRID_W = 64
RMS_EPS = 1e-6
ROPE_THETA = 10000.0
N_MOD = 6
RET_HEADS = 16
ATTN_HEADS = 32
KV_HEADS = 8
TOP_K = 2

V7X_VMEM_BYTES = 64 * 1024 * 1024
V7X_VMEM_LIMIT = V7X_VMEM_BYTES * 7 // 8
LANES = 128
SUBLANES = 8

MM_TM = 1024
MM_TN = 1024
MM_TK = 4096
FFN_TN = 512
FFN_PAD = 1024
ROW_TB = 128
QKV_TB = 256
RET_TT = 1024
ATTN_TQ = 128
ROUTER_TB = 256
MOE_TM = 1024
MOE_SUB = 256
MOE_TN = 512
MOE_TK = 3584

_BF16 = jnp.bfloat16
_F32 = jnp.float32


def _tile(dim, target, quantum):
    best = None
    t = quantum
    while t <= min(dim, target):
        if dim % t == 0:
            best = t
        t += quantum
    return dim if best is None else best


def _params(*sem):
    return pltpu.CompilerParams(dimension_semantics=sem, vmem_limit_bytes=V7X_VMEM_LIMIT)


def _dot(a, b):
    return jnp.dot(a, b, preferred_element_type=_F32)


def _silu(x):
    return x * jax.nn.sigmoid(x)


def _rms(x, g):
    return x * lax.rsqrt(jnp.mean(x * x, axis=-1, keepdims=True) + RMS_EPS) * g


def _modulation_kernel(c_ref, w_ref, b_ref, o_ref):
    s = _silu(c_ref[...]).astype(_BF16)
    o_ref[...] = _dot(s, w_ref[...].astype(_BF16)) + b_ref[...]


def _modulation(cond, w_mod, b_mod):
    depth, d, nd = w_mod.shape
    g = cond.shape[0]
    gp = -(-g // SUBLANES) * SUBLANES
    cond_p = jnp.zeros((gp, d), _F32).at[:g].set(cond)
    tn = _tile(nd, 512, LANES)
    out = pl.pallas_call(
        _modulation_kernel,
        grid=(depth, nd // tn),
        in_specs=[
            pl.BlockSpec((gp, d), lambda l, j: (0, 0)),
            pl.BlockSpec((None, d, tn), lambda l, j: (l, 0, j)),
            pl.BlockSpec((None, 1, tn), lambda l, j: (l, 0, j)),
        ],
        out_specs=pl.BlockSpec((None, gp, tn), lambda l, j: (l, 0, j)),
        out_shape=jax.ShapeDtypeStruct((depth, gp, nd), _F32),
        compiler_params=_params("parallel", "parallel"),
        name="modulation",
    )(cond_p, w_mod, b_mod.reshape(depth, 1, nd))
    return out[:, :g].reshape(depth, g, N_MOD, d)


def _post_kernel(*refs, has_update, gate_idx, has_next, shift_idx, scale_idx):
    it = iter(refs)
    x_ref = next(it)
    if has_update:
        m_ref, gpost_ref, gmod_ref = next(it), next(it), next(it)
    if has_next:
        gnext_ref, nmod_ref = next(it), next(it)
    x = x_ref[...]
    if has_update:
        xo_ref = next(it)
        x = x + gmod_ref[0, pl.ds(gate_idx, 1), :] * _rms(m_ref[...], gpost_ref[...])
        xo_ref[...] = x
    if has_next:
        h = (_rms(x, gnext_ref[...]) * (1.0 + nmod_ref[0, pl.ds(scale_idx, 1), :])
             + nmod_ref[0, pl.ds(shift_idx, 1), :])
        h_ref = next(it)
        h_ref[...] = h.astype(h_ref.dtype)


def _post(x, group_of, *, update=None, nxt=None, h_dtype=_BF16):
    n, d = x.shape
    tb = ROW_TB
    row = pl.BlockSpec((tb, d), lambda i: (i, 0))
    vec = pl.BlockSpec((1, d), lambda i: (0, 0))

    def mod_spec(mods):
        return pl.BlockSpec((1, N_MOD, d), lambda i: (group_of(i * tb), 0, 0))

    args, in_specs, out_specs, out_shape = [x], [row], [], []
    gate_idx = shift_idx = scale_idx = 0
    if update is not None:
        m, g_post, mods, gate_idx = update
        args += [m, g_post.reshape(1, d), mods]
        in_specs += [row, vec, mod_spec(mods)]
        out_specs.append(row)
        out_shape.append(jax.ShapeDtypeStruct((n, d), _F32))
    if nxt is not None:
        g_next, nmods, shift_idx, scale_idx = nxt
        args += [g_next.reshape(1, d), nmods]
        in_specs += [vec, mod_spec(nmods)]
        out_specs.append(row)
        out_shape.append(jax.ShapeDtypeStruct((n, d), h_dtype))
    return pl.pallas_call(
        functools.partial(_post_kernel, has_update=update is not None, gate_idx=gate_idx,
                          has_next=nxt is not None, shift_idx=shift_idx, scale_idx=scale_idx),
        grid=(n // tb,),
        in_specs=in_specs,
        out_specs=out_specs,
        out_shape=out_shape,
        compiler_params=_params("parallel"),
        name="post_norm_modulate",
    )(*args)


def _mm_kernel(x_ref, w_ref, o_ref):
    o_ref[...] = _dot(x_ref[...], w_ref[...]).astype(o_ref.dtype)


def _mm_acc_kernel(x_ref, w_ref, o_ref):
    @pl.when(pl.program_id(2) == 0)
    def _():
        o_ref[...] = jnp.zeros_like(o_ref)

    o_ref[...] += _dot(x_ref[...], w_ref[...])


def _matmul(x, w, out_dtype, *, tk_target=MM_TK, name="matmul"):
    m, k = x.shape
    n = w.shape[1]
    tm, tn, tk = _tile(m, MM_TM, SUBLANES), _tile(n, MM_TN, LANES), _tile(k, tk_target, LANES)
    if tk == k:
        return pl.pallas_call(
            _mm_kernel,
            grid=(m // tm, n // tn),
            in_specs=[pl.BlockSpec((tm, k), lambda i, j: (i, 0)),
                      pl.BlockSpec((k, tn), lambda i, j: (0, j))],
            out_specs=pl.BlockSpec((tm, tn), lambda i, j: (i, j)),
            out_shape=jax.ShapeDtypeStruct((m, n), out_dtype),
            compiler_params=_params("parallel", "parallel"),
            name=name,
        )(x, w)
    assert out_dtype == _F32, "the K-tiled matmul accumulates in its f32 output block"
    return pl.pallas_call(
        _mm_acc_kernel,
        grid=(m // tm, n // tn, k // tk),
        in_specs=[pl.BlockSpec((tm, tk), lambda i, j, kk: (i, kk)),
                  pl.BlockSpec((tk, tn), lambda i, j, kk: (kk, j))],
        out_specs=pl.BlockSpec((tm, tn), lambda i, j, kk: (i, j)),
        out_shape=jax.ShapeDtypeStruct((m, n), _F32),
        compiler_params=_params("parallel", "parallel", "arbitrary"),
        name=name,
    )(x, w)


def _swiglu_kernel(x_ref, w1_ref, w3_ref, o_ref):
    x = x_ref[...]
    o_ref[...] = (_silu(_dot(x, w1_ref[...])) * _dot(x, w3_ref[...])).astype(o_ref.dtype)


def _swiglu_up(x, w1, w3):
    m, k = x.shape
    f = w1.shape[1]
    tm, tn = _tile(m, MM_TM, SUBLANES), _tile(f, FFN_TN, LANES)
    return pl.pallas_call(
        _swiglu_kernel,
        grid=(m // tm, f // tn),
        in_specs=[pl.BlockSpec((tm, k), lambda i, j: (i, 0)),
                  pl.BlockSpec((k, tn), lambda i, j: (0, j)),
                  pl.BlockSpec((k, tn), lambda i, j: (0, j))],
        out_specs=pl.BlockSpec((tm, tn), lambda i, j: (i, j)),
        out_shape=jax.ShapeDtypeStruct((m, f), _BF16),
        compiler_params=_params("parallel", "parallel"),
        name="ffn_swiglu_up",
    )(x, w1, w3)


def _rope_tables(t, d):
    rows = t // GRID_W
    n_freq = d // 4
    inv = ROPE_THETA ** (-jnp.arange(n_freq, dtype=_F32) / n_freq)
    row = jnp.repeat(jnp.arange(rows, dtype=_F32), GRID_W)
    col = jnp.tile(jnp.arange(GRID_W, dtype=_F32), rows)
    ang = jnp.concatenate([row[:, None] * inv, col[:, None] * inv], axis=-1)
    return jnp.cos(ang), jnp.sin(ang)


def _decay_tables(lg, backward):
    L = BLOCK
    r = lax.broadcasted_iota(jnp.int32, (L, L), 0)
    c = lax.broadcasted_iota(jnp.int32, (L, L), 1)
    diff = ((c - r) if backward else (r - c)).astype(_F32)
    intra = jnp.where(diff >= 0, jnp.exp(lg * jnp.maximum(diff, 0.0)), 0.0)
    idx = lax.broadcasted_iota(jnp.int32, (L, 1), 0).astype(_F32)
    pos = (L - 1.0 - idx) if backward else idx
    cross = jnp.exp(lg * (pos + 1.0))
    tail = jnp.exp(lg * (L - 1.0 - pos))
    decay = jnp.exp(jnp.full((1, 1), lg * L, _F32))
    return intra, cross, tail, decay


def _rotate_halves(x, cos, sin):
    half = x.shape[1] // 2
    x1, x2 = x[:, :half], x[:, half:]
    return jnp.concatenate([x1 * cos - x2 * sin, x2 * cos + x1 * sin], axis=-1)


def _retention_chunk(q, k, v, g, s_ref, tables):
    intra, cross, tail, decay = tables
    k = k * (k.shape[1] ** -0.5)
    qb = q.astype(_BF16)
    vb = v.astype(_BF16)
    scores = lax.dot_general(qb, k.astype(_BF16), (((1,), (1,)), ((), ())),
                             preferred_element_type=_F32) * intra
    s = s_ref[...]
    o = _dot(scores.astype(_BF16), vb) + _dot(qb, s.astype(_BF16)) * cross
    kt = (k * tail).T.astype(_BF16)
    s_ref[...] = s * decay + _dot(kt, vb)
    oc = o - jnp.mean(o, axis=-1, keepdims=True)
    on = oc * lax.rsqrt(jnp.mean(oc * oc, axis=-1, keepdims=True) + RMS_EPS)
    return _silu(g) * on


def _retention_kernel(lg_ref, *refs, backward, rope, has_init, has_prev, n_heads):
    it = iter(refs)
    q_ref, k_ref, v_ref, g_ref = next(it), next(it), next(it), next(it)
    if rope:
        cos_ref, sin_ref = next(it), next(it)
    if has_init:
        s0_ref = next(it)
    if has_prev:
        prev_ref = next(it)
    y_ref = next(it)
    s_ref = next(it)

    L = BLOCK
    n_chunks = q_ref.shape[0] // L

    @pl.when(pl.program_id(2) == 0)
    def _():
        s_ref[...] = s0_ref[...] if has_init else jnp.zeros_like(s_ref)

    tables = _decay_tables(lg_ref[(n_heads if backward else 0) + pl.program_id(1)], backward)

    def body(ci, carry):
        cc = (n_chunks - 1 - ci) if backward else ci
        rows = pl.ds(pl.multiple_of(cc * L, L), L)
        q, k = q_ref[rows, :], k_ref[rows, :]
        if rope:
            cos, sin = cos_ref[rows, :], sin_ref[rows, :]
            q, k = _rotate_halves(q, cos, sin), _rotate_halves(k, cos, sin)
        y = _retention_chunk(q, k, v_ref[rows, :], g_ref[rows, :], s_ref, tables)
        if has_prev:
            y = prev_ref[rows, :] + y
        y_ref[rows, :] = y.astype(y_ref.dtype)
        return carry

    lax.fori_loop(0, n_chunks, body, 0)


def _retention_both_kernel(lg_ref, *refs, rope, has_init, has_final, n_heads):
    it = iter(refs)
    q_ref, k_ref, v_ref, gf_ref, gb_ref = next(it), next(it), next(it), next(it), next(it)
    if rope:
        cos_ref, sin_ref = next(it), next(it)
    if has_init:
        s0_ref = next(it)
    y_ref = next(it)
    if has_final:
        fin_ref = next(it)
    sf_ref, sb_ref, yf_ref, yb_ref = next(it), next(it), next(it), next(it)

    L = BLOCK
    n_chunks = q_ref.shape[0] // L
    h = pl.program_id(1)
    sf_ref[...] = s0_ref[0] if has_init else jnp.zeros_like(sf_ref)
    sb_ref[...] = s0_ref[1] if has_init else jnp.zeros_like(sb_ref)
    tab_f = _decay_tables(lg_ref[h], False)
    tab_b = _decay_tables(lg_ref[n_heads + h], True)

    def one(rows, g_ref, s_ref, tables):
        q, k = q_ref[rows, :], k_ref[rows, :]
        if rope:
            cos, sin = cos_ref[rows, :], sin_ref[rows, :]
            q, k = _rotate_halves(q, cos, sin), _rotate_halves(k, cos, sin)
        return _retention_chunk(q, k, v_ref[rows, :], g_ref[rows, :], s_ref, tables)

    def body(ci, carry):
        rf = pl.ds(pl.multiple_of(ci * L, L), L)
        rb = pl.ds(pl.multiple_of((n_chunks - 1 - ci) * L, L), L)
        yf_ref[rf, :] = one(rf, gf_ref, sf_ref, tab_f)
        yb_ref[rb, :] = one(rb, gb_ref, sb_ref, tab_b)
        return carry

    lax.fori_loop(0, n_chunks, body, 0)
    y_ref[...] = (yf_ref[...] + yb_ref[...]).astype(y_ref.dtype)
    if has_final:
        fin_ref[0] = sf_ref[...]
        fin_ref[1] = sb_ref[...]


def _retention_dims(proj):
    nh = RET_HEADS
    dk = proj.shape[1] // (8 * nh)
    return nh, dk, 2 * dk


def _retention_direction(proj, log_gamma, *, row0, n_seq, t, backward, rope_tabs, init_state,
                         prev, out_dtype):
    nh, dk, dv = _retention_dims(proj)
    qk_w, v_w = nh * dk, nh * dv
    tt = _tile(t, RET_TT, BLOCK)
    ntt = t // tt
    rb0 = row0 // tt

    def tile_of(ti):
        return (ntt - 1 - ti) if backward else ti

    def rows(b, ti):
        return rb0 + b * ntt + tile_of(ti)

    gate_col = (2 * qk_w + v_w + (v_w if backward else 0)) // dv
    in_specs = [
        pl.BlockSpec((tt, dk), lambda b, h, ti, lg: (rows(b, ti), h)),
        pl.BlockSpec((tt, dk), lambda b, h, ti, lg: (rows(b, ti), qk_w // dk + h)),
        pl.BlockSpec((tt, dv), lambda b, h, ti, lg: (rows(b, ti), 2 * qk_w // dv + h)),
        pl.BlockSpec((tt, dv), lambda b, h, ti, lg: (rows(b, ti), gate_col + h)),
    ]
    args = [proj, proj, proj, proj]
    if rope_tabs is not None:
        spec = pl.BlockSpec((tt, dk // 2), lambda b, h, ti, lg: (tile_of(ti), 0))
        in_specs += [spec, spec]
        args += list(rope_tabs)
    if init_state is not None:
        in_specs.append(pl.BlockSpec((None, None, None, dk, dv),
                                     lambda b, h, ti, lg: (b, 1 if backward else 0, h, 0, 0)))
        args.append(init_state)
    y_spec = pl.BlockSpec((tt, dv), lambda b, h, ti, lg: (b * ntt + tile_of(ti), h))
    if prev is not None:
        in_specs.append(y_spec)
        args.append(prev)
    return pl.pallas_call(
        functools.partial(_retention_kernel, backward=backward, rope=rope_tabs is not None,
                          has_init=init_state is not None, has_prev=prev is not None, n_heads=nh),
        grid_spec=pltpu.PrefetchScalarGridSpec(
            num_scalar_prefetch=1,
            grid=(n_seq, nh, ntt),
            in_specs=in_specs,
            out_specs=y_spec,
            scratch_shapes=[pltpu.VMEM((dk, dv), _F32)],
        ),
        out_shape=jax.ShapeDtypeStruct((n_seq * t, v_w), out_dtype),
        compiler_params=_params("parallel", "parallel", "arbitrary"),
        name="retention_bwd" if backward else "retention_fwd",
    )(log_gamma.reshape(-1), *args)


def _retention_both(proj, log_gamma, *, row0, n_seq, t, rope_tabs, init_state, want_final):
    nh, dk, dv = _retention_dims(proj)
    qk_w, v_w = nh * dk, nh * dv
    rb0 = row0 // t
    gate_col = (2 * qk_w + v_w) // dv
    in_specs = [
        pl.BlockSpec((t, dk), lambda b, h, lg: (rb0 + b, h)),
        pl.BlockSpec((t, dk), lambda b, h, lg: (rb0 + b, qk_w // dk + h)),
        pl.BlockSpec((t, dv), lambda b, h, lg: (rb0 + b, 2 * qk_w // dv + h)),
        pl.BlockSpec((t, dv), lambda b, h, lg: (rb0 + b, gate_col + h)),
        pl.BlockSpec((t, dv), lambda b, h, lg: (rb0 + b, gate_col + nh + h)),
    ]
    args = [proj] * 5
    if rope_tabs is not None:
        spec = pl.BlockSpec((t, dk // 2), lambda b, h, lg: (0, 0))
        in_specs += [spec, spec]
        args += list(rope_tabs)
    state_spec = pl.BlockSpec((None, 2, None, dk, dv), lambda b, h, lg: (b, 0, h, 0, 0))
    if init_state is not None:
        in_specs.append(state_spec)
        args.append(init_state)
    out_specs = [pl.BlockSpec((t, dv), lambda b, h, lg: (b, h))]
    out_shape = [jax.ShapeDtypeStruct((n_seq * t, v_w), _BF16)]
    if want_final:
        out_specs.append(state_spec)
        out_shape.append(jax.ShapeDtypeStruct((n_seq, 2, nh, dk, dv), _F32))
    res = pl.pallas_call(
        functools.partial(_retention_both_kernel, rope=rope_tabs is not None,
                          has_init=init_state is not None, has_final=want_final, n_heads=nh),
        grid_spec=pltpu.PrefetchScalarGridSpec(
            num_scalar_prefetch=1,
            grid=(n_seq, nh),
            in_specs=in_specs,
            out_specs=out_specs,
            scratch_shapes=[pltpu.VMEM((dk, dv), _F32), pltpu.VMEM((dk, dv), _F32),
                            pltpu.VMEM((t, dv), _F32), pltpu.VMEM((t, dv), _F32)],
        ),
        out_shape=out_shape,
        compiler_params=_params("parallel", "parallel"),
        name="retention_both",
    )(log_gamma.reshape(-1), *args)
    return (res[0], res[1]) if want_final else (res[0], None)


def _retention(proj, decay_logit, *, row0, n_seq, t, rope_tabs, init_state, want_final):
    log_gamma = jax.nn.log_sigmoid(decay_logit.astype(_F32))
    common = dict(row0=row0, n_seq=n_seq, t=t, rope_tabs=rope_tabs, init_state=init_state)
    if t <= RET_TT:
        return _retention_both(proj, log_gamma, want_final=want_final, **common)
    assert not want_final, "final states are only produced by the resident-sequence kernel"
    y_f = _retention_direction(proj, log_gamma, backward=False, prev=None, out_dtype=_F32, **common)
    y = _retention_direction(proj, log_gamma, backward=True, prev=y_f, out_dtype=_BF16, **common)
    return y, None


def _qkv_post_kernel(*refs, n_q, n_kv, hd, rope, want_cache):
    it = iter(refs)
    x_ref, qn_ref, kn_ref = next(it), next(it), next(it)
    if rope:
        cos_ref, sin_ref = next(it), next(it)
    q_ref, k_ref, v_ref = next(it), next(it), next(it)
    if want_cache:
        ck_ref, cv_ref = next(it), next(it)

    def head(col, g_ref):
        x = x_ref[:, col * hd:(col + 1) * hd]
        return x * lax.rsqrt(jnp.mean(x * x, axis=-1, keepdims=True) + RMS_EPS) * g_ref[...]

    def rotate(x):
        if not rope:
            return x
        return x * cos_ref[...] + pltpu.roll(x, hd // 2, 1) * sin_ref[...]

    for hh in range(n_q):
        q_ref[:, hh * hd:(hh + 1) * hd] = rotate(head(hh, qn_ref)).astype(q_ref.dtype)
    for hh in range(n_kv):
        kn = head(n_q + hh, kn_ref)
        v = x_ref[:, (n_q + n_kv + hh) * hd:(n_q + n_kv + hh + 1) * hd]
        if want_cache:
            ck_ref[0, hh] = kn
            cv_ref[0, hh] = v
        k_ref[0, hh] = rotate(kn).astype(k_ref.dtype)
        v_ref[0, hh] = v.astype(v_ref.dtype)


def _qkv_post(qkv, q_norm, k_norm, *, row0, n_seq, t, rope_tabs, want_cache):
    n_q, n_kv = ATTN_HEADS, KV_HEADS
    hd = qkv.shape[1] // (n_q + 2 * n_kv)
    tb = _tile(t, QKV_TB, SUBLANES)
    nt = t // tb
    rb0 = row0 // tb
    in_specs = [pl.BlockSpec((tb, qkv.shape[1]), lambda i: (rb0 + i, 0)),
                pl.BlockSpec((1, hd), lambda i: (0, 0)),
                pl.BlockSpec((1, hd), lambda i: (0, 0))]
    args = [qkv, q_norm.reshape(1, hd), k_norm.reshape(1, hd)]
    if rope_tabs is not None:
        spec = pl.BlockSpec((tb, hd), lambda i: (i % nt, 0))
        in_specs += [spec, spec]
        args += list(rope_tabs)
    head_spec = pl.BlockSpec((1, n_kv, tb, hd), lambda i: (i // nt, 0, i % nt, 0))
    out_specs = [pl.BlockSpec((tb, n_q * hd), lambda i: (i, 0)), head_spec, head_spec]
    out_shape = [jax.ShapeDtypeStruct((n_seq * t, n_q * hd), _BF16),
                 jax.ShapeDtypeStruct((n_seq, n_kv, t, hd), _BF16),
                 jax.ShapeDtypeStruct((n_seq, n_kv, t, hd), _BF16)]
    if want_cache:
        out_specs += [head_spec, head_spec]
        out_shape += [jax.ShapeDtypeStruct((n_seq, n_kv, t, hd), _F32)] * 2
    return pl.pallas_call(
        functools.partial(_qkv_post_kernel, n_q=n_q, n_kv=n_kv, hd=hd,
                          rope=rope_tabs is not None, want_cache=want_cache),
        grid=(n_seq * nt,),
        in_specs=in_specs,
        out_specs=out_specs,
        out_shape=out_shape,
        compiler_params=_params("parallel"),
        name="qkv_norm_rope",
    )(*args)


def _attention_kernel(q_ref, k_ref, v_ref, o_ref, *, group, hd):
    tq = q_ref.shape[0]
    q = jnp.concatenate([q_ref[:, g * hd:(g + 1) * hd] for g in range(group)], axis=0)
    s = lax.dot_general(q, k_ref[...], (((1,), (1,)), ((), ())), preferred_element_type=_F32)
    e = jnp.exp((s - jnp.max(s, axis=-1, keepdims=True)) * (hd ** -0.5))
    inv = 1.0 / jnp.sum(e, axis=-1, keepdims=True)
    o = _dot(e.astype(_BF16), v_ref[...]) * inv
    for g in range(group):
        o_ref[:, g * hd:(g + 1) * hd] = o[g * tq:(g + 1) * tq].astype(o_ref.dtype)


def _attention(q, k, v, *, n_seq, t):
    n_kv, n_keys, hd = k.shape[1], k.shape[2], k.shape[3]
    group = ATTN_HEADS // n_kv
    tq = _tile(t, ATTN_TQ if n_keys > 1024 else 256, SUBLANES)
    nt = t // tq
    q_spec = pl.BlockSpec((tq, group * hd), lambda b, kh, i: (b * nt + i, kh))
    kv_spec = pl.BlockSpec((None, None, n_keys, hd), lambda b, kh, i: (b, kh, 0, 0))
    return pl.pallas_call(
        functools.partial(_attention_kernel, group=group, hd=hd),
        grid=(n_seq, n_kv, nt),
        in_specs=[q_spec, kv_spec, kv_spec],
        out_specs=q_spec,
        out_shape=jax.ShapeDtypeStruct(q.shape, _BF16),
        compiler_params=_params("parallel", "parallel", "parallel"),
        name="attention",
    )(q, k, v)


def _router_kernel(h_ref, w_ref, mi_ref, mf_ref, cnt_ref, run_ref, *, n_exp):
    @pl.when(pl.program_id(0) == 0)
    def _():
        run_ref[...] = jnp.zeros_like(run_ref)

    tb = h_ref.shape[0]
    logits = jnp.dot(h_ref[...], w_ref[...], precision=lax.Precision.HIGHEST,
                     preferred_element_type=_F32)
    lane = lax.broadcasted_iota(jnp.int32, logits.shape, 1).astype(_F32)
    neg = -jnp.inf
    l1 = jnp.where(lane < n_exp, logits, neg)
    m1 = jnp.max(l1, axis=-1, keepdims=True)
    i1 = jnp.min(jnp.where(l1 == m1, lane, float(LANES)), axis=-1, keepdims=True)
    l2 = jnp.where(lane == i1, neg, l1)
    m2 = jnp.max(l2, axis=-1, keepdims=True)
    i2 = jnp.min(jnp.where(l2 == m2, lane, float(LANES)), axis=-1, keepdims=True)
    e2 = jnp.exp(m2 - m1)
    g1 = 1.0 / (1.0 + e2)
    g2 = e2 / (1.0 + e2)
    oh1 = (lane == i1).astype(_F32)
    oh2 = (lane == i2).astype(_F32)
    sel = oh1 + oh2
    r = lax.broadcasted_iota(jnp.int32, (tb, tb), 0)
    c = lax.broadcasted_iota(jnp.int32, (tb, tb), 1)
    before = _dot((c < r).astype(_BF16), sel.astype(_BF16)) + run_ref[...]
    rank1 = jnp.sum(before * oh1, axis=-1, keepdims=True)
    rank2 = jnp.sum(before * oh2, axis=-1, keepdims=True)
    run_ref[...] += jnp.sum(sel, axis=0, keepdims=True)
    cnt_ref[...] = run_ref[...].astype(jnp.int32)
    mi = jnp.where(lane == 0, i1, jnp.where(lane == 1, i2,
                   jnp.where(lane == 2, rank1, jnp.where(lane == 3, rank2, 0.0))))
    mi_ref[...] = mi.astype(jnp.int32)
    mf_ref[...] = jnp.where(lane == 0, g1, jnp.where(lane == 1, g2, 0.0))


def _router(h32, w_router):
    n, d = h32.shape
    n_exp = w_router.shape[1]
    w = jnp.zeros((d, LANES), _F32).at[:, :n_exp].set(w_router)
    tb = _tile(n, ROUTER_TB, SUBLANES)
    row = pl.BlockSpec((tb, LANES), lambda i: (i, 0))
    return pl.pallas_call(
        functools.partial(_router_kernel, n_exp=n_exp),
        grid=(n // tb,),
        in_specs=[pl.BlockSpec((tb, d), lambda i: (i, 0)), pl.BlockSpec((d, LANES), lambda i: (0, 0))],
        out_specs=[row, row, pl.BlockSpec((1, LANES), lambda i: (0, 0))],
        out_shape=[jax.ShapeDtypeStruct((n, LANES), jnp.int32),
                   jax.ShapeDtypeStruct((n, LANES), _F32),
                   jax.ShapeDtypeStruct((1, LANES), jnp.int32)],
        scratch_shapes=[pltpu.VMEM((1, LANES), _F32)],
        compiler_params=_params("arbitrary"),
        name="moe_router",
    )(h32, w)


def _invert_kernel(pos_ref, inv_ref, *, n_tok):
    def clear(r, carry):
        inv_ref[r] = 0
        return carry

    lax.fori_loop(0, inv_ref.shape[0], clear, 0)

    def put(i, carry):
        inv_ref[pos_ref[i]] = jnp.where(i >= n_tok, i - n_tok, i)
        return carry

    lax.fori_loop(0, pos_ref.shape[0], put, 0)


def _invert(pos, n_rows, n_tok):
    return pl.pallas_call(
        functools.partial(_invert_kernel, n_tok=n_tok),
        in_specs=[pl.BlockSpec(memory_space=pltpu.SMEM)],
        out_specs=pl.BlockSpec(memory_space=pltpu.SMEM),
        out_shape=jax.ShapeDtypeStruct((n_rows,), jnp.int32),
        name="moe_invert",
    )(pos)


def _dispatch_kernel(inv_ref, live_ref, h_hbm, o_ref, buf, sem):
    c = pl.program_id(0)
    ch = buf.shape[0]
    live = live_ref[c]

    def row_copy(k, src):
        return pltpu.make_async_copy(h_hbm.at[pl.ds(src, 1), :], buf.at[pl.ds(k, 1), :], sem.at[0])

    @pl.when(live == 0)
    def _():
        o_ref[...] = jnp.zeros_like(o_ref)

    @pl.when(live > 0)
    def _():
        def issue(k, carry):
            row_copy(k, inv_ref[c * ch + k]).start()
            return carry

        def drain(k, carry):
            row_copy(k, 0).wait()
            return carry

        lax.fori_loop(0, ch, issue, 0)
        lax.fori_loop(0, ch, drain, 0)
        rows = lax.broadcasted_iota(jnp.int32, (ch, 1), 0)
        o_ref[...] = jnp.where(rows < live, buf[...], 0.0).astype(o_ref.dtype)


def _dispatch(h32, inv, chunk_live, chunk):
    n, d = h32.shape
    n_rows = inv.shape[0]
    return pl.pallas_call(
        _dispatch_kernel,
        grid_spec=pltpu.PrefetchScalarGridSpec(
            num_scalar_prefetch=2,
            grid=(n_rows // chunk,),
            in_specs=[pl.BlockSpec(memory_space=pl.ANY)],
            out_specs=pl.BlockSpec((chunk, d), lambda c, inv, live: (c, 0)),
            scratch_shapes=[pltpu.VMEM((chunk, d), _F32), pltpu.SemaphoreType.DMA((1,))],
        ),
        out_shape=jax.ShapeDtypeStruct((n_rows, d), _BF16),
        compiler_params=_params("arbitrary"),
        name="moe_dispatch",
    )(inv, chunk_live, h32)


def _group_swiglu_kernel(te_ref, ns_ref, x_ref, w1_ref, w3_ref, o_ref, *, sub):
    del te_ref
    n_sub = ns_ref[pl.program_id(0)]
    full = x_ref.shape[0] // sub

    def act(x):
        return (_silu(_dot(x, w1_ref[...])) * _dot(x, w3_ref[...])).astype(o_ref.dtype)

    @pl.when(n_sub == full)
    def _():
        o_ref[...] = act(x_ref[...])

    @pl.when(n_sub < full)
    def _():
        o_ref[...] = jnp.zeros_like(o_ref)

        def body(s, carry):
            rows = pl.ds(pl.multiple_of(s * sub, sub), sub)
            o_ref[rows, :] = act(x_ref[rows, :])
            return carry

        lax.fori_loop(0, n_sub, body, 0)


def _group_down_kernel(te_ref, ns_ref, x_ref, w_ref, o_ref, *, sub):
    del te_ref
    n_sub = ns_ref[pl.program_id(0)]
    full = x_ref.shape[0] // sub

    @pl.when(pl.program_id(2) == 0)
    def _():
        o_ref[...] = jnp.zeros_like(o_ref)

    @pl.when(n_sub == full)
    def _():
        o_ref[...] += _dot(x_ref[...], w_ref[...])

    @pl.when(n_sub < full)
    def _():
        def body(s, carry):
            rows = pl.ds(pl.multiple_of(s * sub, sub), sub)
            o_ref[rows, :] += _dot(x_ref[rows, :], w_ref[...])
            return carry

        lax.fori_loop(0, n_sub, body, 0)


def _grouped_experts(xg, tile_expert, tile_nsub, w1, w3, w2, *, tm, sub):
    r, d = xg.shape
    f = w1.shape[2]
    nt = r // tm
    tn = _tile(f, MOE_TN, LANES)
    a = pl.pallas_call(
        functools.partial(_group_swiglu_kernel, sub=sub),
        grid_spec=pltpu.PrefetchScalarGridSpec(
            num_scalar_prefetch=2,
            grid=(nt, f // tn),
            in_specs=[pl.BlockSpec((tm, d), lambda i, j, te, ns: (i, 0)),
                      pl.BlockSpec((None, d, tn), lambda i, j, te, ns: (te[i], 0, j)),
                      pl.BlockSpec((None, d, tn), lambda i, j, te, ns: (te[i], 0, j))],
            out_specs=pl.BlockSpec((tm, tn), lambda i, j, te, ns: (i, j)),
        ),
        out_shape=jax.ShapeDtypeStruct((r, f), _BF16),
        compiler_params=_params("parallel", "parallel"),
        name="moe_swiglu_up",
    )(tile_expert, tile_nsub, xg, w1, w3)
    tn2, tk = _tile(d, MM_TN, LANES), _tile(f, MOE_TK, LANES)
    return pl.pallas_call(
        functools.partial(_group_down_kernel, sub=sub),
        grid_spec=pltpu.PrefetchScalarGridSpec(
            num_scalar_prefetch=2,
            grid=(nt, d // tn2, f // tk),
            in_specs=[pl.BlockSpec((tm, tk), lambda i, j, k, te, ns: (i, k)),
                      pl.BlockSpec((None, tk, tn2), lambda i, j, k, te, ns: (te[i], k, j))],
            out_specs=pl.BlockSpec((tm, tn2), lambda i, j, k, te, ns: (i, j)),
        ),
        out_shape=jax.ShapeDtypeStruct((r, d), _F32),
        compiler_params=_params("parallel", "parallel", "arbitrary"),
        name="moe_down",
    )(tile_expert, tile_nsub, a, w2)


def _combine_kernel(p1_ref, p2_ref, y_hbm, gate_ref, x_ref, g_ref, mod_ref, o_ref, buf1, buf2, sem,
                    *, gate_idx):
    c = pl.program_id(0)
    ch = buf1.shape[0]

    def row_copy(slot, k, src):
        buf = buf1 if slot == 0 else buf2
        return pltpu.make_async_copy(y_hbm.at[pl.ds(src, 1), :], buf.at[pl.ds(k, 1), :], sem.at[slot])

    def issue(k, carry):
        row_copy(0, k, p1_ref[c * ch + k]).start()
        row_copy(1, k, p2_ref[c * ch + k]).start()
        return carry

    def drain(k, carry):
        row_copy(0, k, 0).wait()
        row_copy(1, k, 0).wait()
        return carry

    lax.fori_loop(0, ch, issue, 0)
    lax.fori_loop(0, ch, drain, 0)
    gates = gate_ref[...]
    f = gates[:, 0:1] * buf1[...] + gates[:, 1:2] * buf2[...]
    o_ref[...] = x_ref[...] + mod_ref[0, pl.ds(gate_idx, 1), :] * _rms(f, g_ref[...])


def _combine(y, pos1, pos2, gates, x, g_post, mods, gate_idx, group_of, chunk):
    n, d = x.shape
    row = pl.BlockSpec((chunk, d), lambda c, p1, p2: (c, 0))
    return pl.pallas_call(
        functools.partial(_combine_kernel, gate_idx=gate_idx),
        grid_spec=pltpu.PrefetchScalarGridSpec(
            num_scalar_prefetch=2,
            grid=(n // chunk,),
            in_specs=[pl.BlockSpec(memory_space=pl.ANY),
                      pl.BlockSpec((chunk, LANES), lambda c, p1, p2: (c, 0)),
                      row,
                      pl.BlockSpec((1, d), lambda c, p1, p2: (0, 0)),
                      pl.BlockSpec((1, N_MOD, d), lambda c, p1, p2: (group_of(c * chunk), 0, 0))],
            out_specs=row,
            scratch_shapes=[pltpu.VMEM((chunk, d), _F32), pltpu.VMEM((chunk, d), _F32),
                            pltpu.SemaphoreType.DMA((2,))],
        ),
        out_shape=jax.ShapeDtypeStruct((n, d), _F32),
        compiler_params=_params("arbitrary"),
        name="moe_combine",
    )(pos1, pos2, y, gates, x, g_post.reshape(1, d), mods)


def _moe(h32, x, w_router, w1, w3, w2, g_post, mods, gate_idx, group_of):
    n, d = h32.shape
    n_exp = w_router.shape[1]
    tm = _tile(TOP_K * n, MOE_TM, SUBLANES)
    sub = _tile(tm, MOE_SUB, SUBLANES)
    meta_i, meta_f, counts = _router(h32, w_router)
    cnt = counts[0, :n_exp]
    padded = (cnt + tm - 1) // tm * tm
    ends = jnp.cumsum(padded)
    offs = ends - padded
    pos1 = (offs[meta_i[:, 0]] + meta_i[:, 2]).astype(jnp.int32)
    pos2 = (offs[meta_i[:, 1]] + meta_i[:, 3]).astype(jnp.int32)
    n_rows = TOP_K * n + n_exp * tm

    def expert_of(start):
        return jnp.minimum(jnp.sum(ends[None, :] <= start[:, None], axis=1), n_exp - 1)

    def live_rows(start, width):
        e = expert_of(start)
        return jnp.clip(cnt[e] - (start - offs[e]), 0, width).astype(jnp.int32)

    tile_start = jnp.arange(n_rows // tm, dtype=jnp.int32) * tm
    tile_expert = expert_of(tile_start).astype(jnp.int32)
    tile_nsub = (live_rows(tile_start, tm) + sub - 1) // sub
    chunk_live = live_rows(jnp.arange(n_rows // sub, dtype=jnp.int32) * sub, sub)
    inv = _invert(jnp.concatenate([pos1, pos2]), n_rows, n)
    xg = _dispatch(h32, inv, chunk_live, sub)
    y = _grouped_experts(xg, tile_expert, tile_nsub, w1, w3, w2, tm=tm, sub=sub)
    return _combine(y, pos1, pos2, meta_f, x, g_post, mods, gate_idx, group_of,
                    _tile(n, MOE_SUB, SUBLANES))


def kernel(x_prompt, x_sample, state_ret, cache_k, cache_v, c, c_ctx, w_mod, b_mod, norm_g,
           ret_w_in, ret_decay_logit, ret_w_out, attn_w_qkv, attn_q_norm, attn_k_norm, attn_w_o,
           ffn_w1, ffn_w3, ffn_w2, moe_router, moe_w1, moe_w3, moe_w2):
    bp, sp, d = x_prompt.shape
    bs, ss, _ = x_sample.shape
    n_p, n_s = bp * sp, bs * ss
    depth = w_mod.shape[0]

    def group_of(row):
        return jnp.where(row < n_p, 0, 1 + (row - n_p) // ss)

    x = jnp.concatenate([x_prompt.reshape(n_p, d), x_sample.reshape(n_s, d)], axis=0)
    mods = _modulation(jnp.concatenate([c_ctx[None, :], c], axis=0), w_mod, b_mod)

    new_ret, new_k, new_v = [], [], []
    (h,) = _post(x, group_of, nxt=(norm_g[0, 0], mods[0], 0, 1))
    for i in range(depth):
        j = i // 2
        md = mods[i]
        last = i == depth - 1
        if i % 2 == 0:
            proj = _matmul(h, ret_w_in[j].astype(_BF16), _F32, name="ret_in_proj")
            dk = proj.shape[1] // (8 * RET_HEADS)
            y_p, st = _retention(proj, ret_decay_logit[j], row0=0, n_seq=bp, t=sp, rope_tabs=None,
                                 init_state=None, want_final=True)
            y_s, _ = _retention(proj, ret_decay_logit[j], row0=n_p, n_seq=bs, t=ss,
                                rope_tabs=_rope_tables(ss, dk), init_state=state_ret[:, j],
                                want_final=False)
            new_ret.append(st)
            mix = _matmul(jnp.concatenate([y_p, y_s], axis=0), ret_w_out[j].astype(_BF16), _F32,
                          name="ret_out_proj")
        else:
            qkv = _matmul(h, attn_w_qkv[j].astype(_BF16), _F32, name="attn_qkv_proj")
            hd = qkv.shape[1] // (ATTN_HEADS + 2 * KV_HEADS)
            q_p, k_p, v_p, ck, cv = _qkv_post(qkv, attn_q_norm[j], attn_k_norm[j], row0=0, n_seq=bp,
                                              t=sp, rope_tabs=None, want_cache=True)
            cos, sin = _rope_tables(ss, hd)
            tabs = (jnp.concatenate([cos, cos], axis=-1), jnp.concatenate([-sin, sin], axis=-1))
            q_s, k_s, v_s = _qkv_post(qkv, attn_q_norm[j], attn_k_norm[j], row0=n_p, n_seq=bs, t=ss,
                                      rope_tabs=tabs, want_cache=False)
            new_k.append(ck)
            new_v.append(cv)
            o_p = _attention(q_p, k_p, v_p, n_seq=bp, t=sp)
            o_s = _attention(q_s,
                             jnp.concatenate([cache_k[:, j].astype(_BF16), k_s], axis=2),
                             jnp.concatenate([cache_v[:, j].astype(_BF16), v_s], axis=2),
                             n_seq=bs, t=ss)
            mix = _matmul(jnp.concatenate([o_p, o_s], axis=0), attn_w_o[j].astype(_BF16), _F32,
                          name="attn_out_proj")
        moe_layer = i % 2 == 1
        x, h = _post(x, group_of, update=(mix, norm_g[i, 1], md, 2), nxt=(norm_g[i, 2], md, 3, 4),
                     h_dtype=_F32 if moe_layer else _BF16)
        if not moe_layer:
            f = ffn_w1.shape[2]
            fp = -(-f // FFN_PAD) * FFN_PAD
            pad_c = lambda w: jnp.pad(w.astype(_BF16), ((0, 0), (0, fp - f)))
            up = _swiglu_up(h, pad_c(ffn_w1[j]), pad_c(ffn_w3[j]))
            w2 = jnp.pad(ffn_w2[j].astype(_BF16), ((0, fp - f), (0, 0)))
            ff = _matmul(up, w2, _F32, tk_target=fp // 4, name="ffn_down")
            nxt = None if last else (norm_g[i + 1, 0], mods[i + 1], 0, 1)
            res = _post(x, group_of, update=(ff, norm_g[i, 3], md, 5), nxt=nxt)
            x = res[0]
            h = None if last else res[1]
        else:
            x = _moe(h, x, moe_router[j], moe_w1[j].astype(_BF16), moe_w3[j].astype(_BF16),
                     moe_w2[j].astype(_BF16), norm_g[i, 3], md, 5, group_of)
            if not last:
                (h,) = _post(x, group_of, nxt=(norm_g[i + 1, 0], mods[i + 1], 0, 1))

    return (x[:n_p].reshape(bp, sp, d), x[n_p:].reshape(bs, ss, d),
            jnp.stack(new_ret, axis=1), jnp.stack(new_k, axis=1), jnp.stack(new_v, axis=1))
```

```python
import functools

import jax
import jax.numpy as jnp
from jax import lax
from jax.experimental import pallas as pl
from jax.experimental.pallas import tpu as pltpu

BLOCK = 128
GRID_W = 64
RMS_EPS = 1e-6
ROPE_THETA = 10000.0
N_MOD = 6
RET_HEADS = 16
ATTN_HEADS = 32
KV_HEADS = 8
TOP_K = 2

V7X_VMEM_BYTES = 64 * 1024 * 1024
V7X_VMEM_LIMIT = V7X_VMEM_BYTES * 7 // 8
LANES = 128
SUBLANES = 8

MM_TM = 1024
MM_TN = 1024
MM_TK = 4096
FFN_TN = 512
FFN_PAD = 1024
ROW_TB = 128
QKV_TB = 256
RET_TT = 1024
RET_HEADS_PER_STEP = 2
ATTN_TQ = 256
ROUTER_TB = 256
MOE_TM = 1024
MOE_SUB = 256
MOE_TN = 512
MOE_TK = 1024
MOE_DOWN_TN = 2048
ROW_LOOP_UNROLL = 8

_BF16 = jnp.bfloat16
_F32 = jnp.float32


def _tile(dim, target, quantum):
    best = None
    t = quantum
    while t <= min(dim, target):
        if dim % t == 0:
            best = t
        t += quantum
    return dim if best is None else best


def _params(*sem):
    return pltpu.CompilerParams(dimension_semantics=sem, vmem_limit_bytes=V7X_VMEM_LIMIT)


def _dot(a, b):
    return jnp.dot(a, b, preferred_element_type=_F32)


def _silu(x):
    return x * jax.nn.sigmoid(x)


def _rms(x, g):
    return x * lax.rsqrt(jnp.mean(x * x, axis=-1, keepdims=True) + RMS_EPS) * g


def _modulation_kernel(c_ref, w_ref, b_ref, o_ref):
    s = _silu(c_ref[...]).astype(_BF16)
    o_ref[...] = _dot(s, w_ref[...].astype(_BF16)) + b_ref[...]


def _modulation(cond, w_mod, b_mod):
    depth, d, nd = w_mod.shape
    g = cond.shape[0]
    gp = -(-g // SUBLANES) * SUBLANES
    cond_p = jnp.zeros((gp, d), _F32).at[:g].set(cond)
    tn = _tile(nd, 512, LANES)
    out = pl.pallas_call(
        _modulation_kernel,
        grid=(depth, nd // tn),
        in_specs=[
            pl.BlockSpec((gp, d), lambda l, j: (0, 0)),
            pl.BlockSpec((None, d, tn), lambda l, j: (l, 0, j)),
            pl.BlockSpec((None, 1, tn), lambda l, j: (l, 0, j)),
        ],
        out_specs=pl.BlockSpec((None, gp, tn), lambda l, j: (l, 0, j)),
        out_shape=jax.ShapeDtypeStruct((depth, gp, nd), _F32),
        compiler_params=_params("parallel", "parallel"),
        name="modulation",
    )(cond_p, w_mod, b_mod.reshape(depth, 1, nd))
    return out[:, :g].reshape(depth, g, N_MOD, d)


def _post_kernel(*refs, has_update, gate_idx, has_next, shift_idx, scale_idx):
    it = iter(refs)
    x_ref = next(it)
    if has_update:
        m_ref, gpost_ref, gmod_ref = next(it), next(it), next(it)
    if has_next:
        gnext_ref, nmod_ref = next(it), next(it)
    x = x_ref[...]
    if has_update:
        xo_ref = next(it)
        x = x + gmod_ref[0, pl.ds(gate_idx, 1), :] * _rms(m_ref[...], gpost_ref[...])
        xo_ref[...] = x
    if has_next:
        h = (_rms(x, gnext_ref[...]) * (1.0 + nmod_ref[0, pl.ds(scale_idx, 1), :])
             + nmod_ref[0, pl.ds(shift_idx, 1), :])
        h_ref = next(it)
        h_ref[...] = h.astype(h_ref.dtype)


def _post(x, group_of, *, update=None, nxt=None, h_dtype=_BF16):
    n, d = x.shape
    tb = ROW_TB
    row = pl.BlockSpec((tb, d), lambda i: (i, 0))
    vec = pl.BlockSpec((1, d), lambda i: (0, 0))

    def mod_spec(mods):
        return pl.BlockSpec((1, N_MOD, d), lambda i: (group_of(i * tb), 0, 0))

    args, in_specs, out_specs, out_shape = [x], [row], [], []
    gate_idx = shift_idx = scale_idx = 0
    if update is not None:
        m, g_post, mods, gate_idx = update
        args += [m, g_post.reshape(1, d), mods]
        in_specs += [row, vec, mod_spec(mods)]
        out_specs.append(row)
        out_shape.append(jax.ShapeDtypeStruct((n, d), _F32))
    if nxt is not None:
        g_next, nmods, shift_idx, scale_idx = nxt
        args += [g_next.reshape(1, d), nmods]
        in_specs += [vec, mod_spec(nmods)]
        out_specs.append(row)
        out_shape.append(jax.ShapeDtypeStruct((n, d), h_dtype))
    return pl.pallas_call(
        functools.partial(_post_kernel, has_update=update is not None, gate_idx=gate_idx,
                          has_next=nxt is not None, shift_idx=shift_idx, scale_idx=scale_idx),
        grid=(n // tb,),
        in_specs=in_specs,
        out_specs=out_specs,
        out_shape=out_shape,
        compiler_params=_params("parallel"),
        name="post_norm_modulate",
    )(*args)


def _mm_kernel(x_ref, w_ref, o_ref):
    o_ref[...] = _dot(x_ref[...], w_ref[...]).astype(o_ref.dtype)


def _mm_acc_kernel(x_ref, w_ref, o_ref):
    @pl.when(pl.program_id(2) == 0)
    def _():
        o_ref[...] = jnp.zeros_like(o_ref)

    o_ref[...] += _dot(x_ref[...], w_ref[...])


def _matmul(x, w, out_dtype, *, tk_target=MM_TK, name="matmul"):
    m, k = x.shape
    n = w.shape[1]
    tm, tn, tk = _tile(m, MM_TM, SUBLANES), _tile(n, MM_TN, LANES), _tile(k, tk_target, LANES)
    if tk == k:
        return pl.pallas_call(
            _mm_kernel,
            grid=(m // tm, n // tn),
            in_specs=[pl.BlockSpec((tm, k), lambda i, j: (i, 0)),
                      pl.BlockSpec((k, tn), lambda i, j: (0, j))],
            out_specs=pl.BlockSpec((tm, tn), lambda i, j: (i, j)),
            out_shape=jax.ShapeDtypeStruct((m, n), out_dtype),
            compiler_params=_params("parallel", "parallel"),
            name=name,
        )(x, w)
    assert out_dtype == _F32, "the K-tiled matmul accumulates in its f32 output block"
    return pl.pallas_call(
        _mm_acc_kernel,
        grid=(m // tm, n // tn, k // tk),
        in_specs=[pl.BlockSpec((tm, tk), lambda i, j, kk: (i, kk)),
                  pl.BlockSpec((tk, tn), lambda i, j, kk: (kk, j))],
        out_specs=pl.BlockSpec((tm, tn), lambda i, j, kk: (i, j)),
        out_shape=jax.ShapeDtypeStruct((m, n), _F32),
        compiler_params=_params("parallel", "parallel", "arbitrary"),
        name=name,
    )(x, w)


def _swiglu_kernel(x_ref, w1_ref, w3_ref, o_ref):
    x = x_ref[...]
    o_ref[...] = (_silu(_dot(x, w1_ref[...])) * _dot(x, w3_ref[...])).astype(o_ref.dtype)


def _swiglu_up(x, w1, w3):
    m, k = x.shape
    f = w1.shape[1]
    tm, tn = _tile(m, MM_TM, SUBLANES), _tile(f, FFN_TN, LANES)
    return pl.pallas_call(
        _swiglu_kernel,
        grid=(m // tm, f // tn),
        in_specs=[pl.BlockSpec((tm, k), lambda i, j: (i, 0)),
                  pl.BlockSpec((k, tn), lambda i, j: (0, j)),
                  pl.BlockSpec((k, tn), lambda i, j: (0, j))],
        out_specs=pl.BlockSpec((tm, tn), lambda i, j: (i, j)),
        out_shape=jax.ShapeDtypeStruct((m, f), _BF16),
        compiler_params=_params("parallel", "parallel"),
        name="ffn_swiglu_up",
    )(x, w1, w3)


def _rope_tables(t, d):
    rows = t // GRID_W
    n_freq = d // 4
    inv = ROPE_THETA ** (-jnp.arange(n_freq, dtype=_F32) / n_freq)
    row = jnp.repeat(jnp.arange(rows, dtype=_F32), GRID_W)
    col = jnp.tile(jnp.arange(GRID_W, dtype=_F32), rows)
    ang = jnp.concatenate([row[:, None] * inv, col[:, None] * inv], axis=-1)
    return jnp.cos(ang), jnp.sin(ang)


def _decay_tables(lg, backward):
    L = BLOCK
    r = lax.broadcasted_iota(jnp.int32, (L, L), 0)
    c = lax.broadcasted_iota(jnp.int32, (L, L), 1)
    diff = ((c - r) if backward else (r - c)).astype(_F32)
    intra = jnp.where(diff >= 0, jnp.exp(lg * jnp.maximum(diff, 0.0)), 0.0)
    idx = lax.broadcasted_iota(jnp.int32, (L, 1), 0).astype(_F32)
    pos = (L - 1.0 - idx) if backward else idx
    cross = jnp.exp(lg * (pos + 1.0))
    tail = jnp.exp(lg * (L - 1.0 - pos))
    decay = jnp.exp(jnp.full((1, 1), lg * L, _F32))
    return intra, cross, tail, decay


def _rotate_halves(x, cos, sin):
    half = x.shape[1] // 2
    x1, x2 = x[:, :half], x[:, half:]
    return jnp.concatenate([x1 * cos - x2 * sin, x2 * cos + x1 * sin], axis=-1)


def _retention_chunk(q, k, v, g, s_ref, tables):
    intra, cross, tail, decay = tables
    k = k * (k.shape[1] ** -0.5)
    qb = q.astype(_BF16)
    vb = v.astype(_BF16)
    scores = lax.dot_general(qb, k.astype(_BF16), (((1,), (1,)), ((), ())),
                             preferred_element_type=_F32) * intra
    s = s_ref[...]
    o = _dot(scores.astype(_BF16), vb) + _dot(qb, s.astype(_BF16)) * cross
    kt = (k * tail).T.astype(_BF16)
    s_ref[...] = s * decay + _dot(kt, vb)
    oc = o - jnp.mean(o, axis=-1, keepdims=True)
    on = oc * lax.rsqrt(jnp.mean(oc * oc, axis=-1, keepdims=True) + RMS_EPS)
    return _silu(g) * on


def _retention_kernel(lg_ref, *refs, backward, rope, has_init, has_prev, n_heads, hp):
    it = iter(refs)
    q_ref, k_ref, v_ref, g_ref = next(it), next(it), next(it), next(it)
    if rope:
        cos_ref, sin_ref = next(it), next(it)
    if has_init:
        s0_ref = next(it)
    if has_prev:
        prev_ref = next(it)
    y_ref = next(it)
    s_ref = next(it)

    L = BLOCK
    n_chunks = q_ref.shape[0] // L

    @pl.when(pl.program_id(2) == 0)
    def _():
        s_ref[...] = s0_ref[...] if has_init else jnp.zeros_like(s_ref)

    dk, dv = q_ref.shape[1] // hp, v_ref.shape[1] // hp
    lg0 = (n_heads if backward else 0) + pl.program_id(1) * hp
    tables = [_decay_tables(lg_ref[lg0 + p], backward) for p in range(hp)]

    def body(ci, carry):
        cc = (n_chunks - 1 - ci) if backward else ci
        rows = pl.ds(pl.multiple_of(cc * L, L), L)
        for p in range(hp):
            qc, vc = slice(p * dk, (p + 1) * dk), slice(p * dv, (p + 1) * dv)
            q, k = q_ref[rows, qc], k_ref[rows, qc]
            if rope:
                cos, sin = cos_ref[rows, :], sin_ref[rows, :]
                q, k = _rotate_halves(q, cos, sin), _rotate_halves(k, cos, sin)
            y = _retention_chunk(q, k, v_ref[rows, vc], g_ref[rows, vc], s_ref.at[p], tables[p])
            if has_prev:
                y = prev_ref[rows, vc] + y
            y_ref[rows, vc] = y.astype(y_ref.dtype)
        return carry

    lax.fori_loop(0, n_chunks, body, 0)


def _retention_both_kernel(lg_ref, *refs, rope, has_init, has_final, n_heads, hp):
    it = iter(refs)
    q_ref, k_ref, v_ref, gf_ref, gb_ref = next(it), next(it), next(it), next(it), next(it)
    if rope:
        cos_ref, sin_ref = next(it), next(it)
    if has_init:
        s0_ref = next(it)
    y_ref = next(it)
    if has_final:
        fin_ref = next(it)
    sf_ref, sb_ref, yf_ref, yb_ref = next(it), next(it), next(it), next(it)

    L = BLOCK
    n_chunks = q_ref.shape[0] // L
    dk, dv = q_ref.shape[1] // hp, v_ref.shape[1] // hp
    h0 = pl.program_id(1) * hp
    sf_ref[...] = s0_ref[0] if has_init else jnp.zeros_like(sf_ref)
    sb_ref[...] = s0_ref[1] if has_init else jnp.zeros_like(sb_ref)
    tab_f = [_decay_tables(lg_ref[h0 + p], False) for p in range(hp)]
    tab_b = [_decay_tables(lg_ref[n_heads + h0 + p], True) for p in range(hp)]

    def one(rows, p, g_ref, s_ref, tables):
        qc, vc = slice(p * dk, (p + 1) * dk), slice(p * dv, (p + 1) * dv)
        q, k = q_ref[rows, qc], k_ref[rows, qc]
        if rope:
            cos, sin = cos_ref[rows, :], sin_ref[rows, :]
            q, k = _rotate_halves(q, cos, sin), _rotate_halves(k, cos, sin)
        return _retention_chunk(q, k, v_ref[rows, vc], g_ref[rows, vc], s_ref.at[p], tables[p])

    def body(ci, carry):
        rf = pl.ds(pl.multiple_of(ci * L, L), L)
        rb = pl.ds(pl.multiple_of((n_chunks - 1 - ci) * L, L), L)
        for p in range(hp):
            vc = slice(p * dv, (p + 1) * dv)
            yf_ref[rf, vc] = one(rf, p, gf_ref, sf_ref, tab_f)
            yb_ref[rb, vc] = one(rb, p, gb_ref, sb_ref, tab_b)
        return carry

    lax.fori_loop(0, n_chunks, body, 0)
    y_ref[...] = (yf_ref[...] + yb_ref[...]).astype(y_ref.dtype)
    if has_final:
        fin_ref[0] = sf_ref[...]
        fin_ref[1] = sb_ref[...]


def _retention_dims(proj):
    nh = RET_HEADS
    dk = proj.shape[1] // (8 * nh)
    return nh, dk, 2 * dk


def _retention_direction(proj, log_gamma, *, row0, n_seq, t, backward, rope_tabs, init_state,
                         prev, out_dtype):
    nh, dk, dv = _retention_dims(proj)
    qk_w, v_w = nh * dk, nh * dv
    tt = _tile(t, RET_TT, BLOCK)
    ntt = t // tt
    rb0 = row0 // tt

    def tile_of(ti):
        return (ntt - 1 - ti) if backward else ti

    def rows(b, ti):
        return rb0 + b * ntt + tile_of(ti)

    hp = RET_HEADS_PER_STEP if nh % RET_HEADS_PER_STEP == 0 else 1
    wk, wv = hp * dk, hp * dv
    gate_col = (2 * qk_w + v_w + (v_w if backward else 0)) // wv
    in_specs = [
        pl.BlockSpec((tt, wk), lambda b, h, ti, lg: (rows(b, ti), h)),
        pl.BlockSpec((tt, wk), lambda b, h, ti, lg: (rows(b, ti), qk_w // wk + h)),
        pl.BlockSpec((tt, wv), lambda b, h, ti, lg: (rows(b, ti), 2 * qk_w // wv + h)),
        pl.BlockSpec((tt, wv), lambda b, h, ti, lg: (rows(b, ti), gate_col + h)),
    ]
    args = [proj, proj, proj, proj]
    if rope_tabs is not None:
        spec = pl.BlockSpec((tt, dk // 2), lambda b, h, ti, lg: (tile_of(ti), 0))
        in_specs += [spec, spec]
        args += list(rope_tabs)
    if init_state is not None:
        in_specs.append(pl.BlockSpec((None, None, hp, dk, dv),
                                     lambda b, h, ti, lg: (b, 1 if backward else 0, h, 0, 0)))
        args.append(init_state)
    y_spec = pl.BlockSpec((tt, wv), lambda b, h, ti, lg: (b * ntt + tile_of(ti), h))
    if prev is not None:
        in_specs.append(y_spec)
        args.append(prev)
    return pl.pallas_call(
        functools.partial(_retention_kernel, backward=backward, rope=rope_tabs is not None,
                          has_init=init_state is not None, has_prev=prev is not None, n_heads=nh,
                          hp=hp),
        grid_spec=pltpu.PrefetchScalarGridSpec(
            num_scalar_prefetch=1,
            grid=(n_seq, nh // hp, ntt),
            in_specs=in_specs,
            out_specs=y_spec,
            scratch_shapes=[pltpu.VMEM((hp, dk, dv), _F32)],
        ),
        out_shape=jax.ShapeDtypeStruct((n_seq * t, v_w), out_dtype),
        compiler_params=_params("parallel", "parallel", "arbitrary"),
        name="retention_bwd" if backward else "retention_fwd",
    )(log_gamma.reshape(-1), *args)


def _retention_both(proj, log_gamma, *, row0, n_seq, t, rope_tabs, init_state, want_final):
    nh, dk, dv = _retention_dims(proj)
    qk_w, v_w = nh * dk, nh * dv
    rb0 = row0 // t
    hp = RET_HEADS_PER_STEP if nh % RET_HEADS_PER_STEP == 0 else 1
    wk, wv = hp * dk, hp * dv
    gate_col = (2 * qk_w + v_w) // wv
    in_specs = [
        pl.BlockSpec((t, wk), lambda b, h, lg: (rb0 + b, h)),
        pl.BlockSpec((t, wk), lambda b, h, lg: (rb0 + b, qk_w // wk + h)),
        pl.BlockSpec((t, wv), lambda b, h, lg: (rb0 + b, 2 * qk_w // wv + h)),
        pl.BlockSpec((t, wv), lambda b, h, lg: (rb0 + b, gate_col + h)),
        pl.BlockSpec((t, wv), lambda b, h, lg: (rb0 + b, gate_col + nh // hp + h)),
    ]
    args = [proj] * 5
    if rope_tabs is not None:
        spec = pl.BlockSpec((t, dk // 2), lambda b, h, lg: (0, 0))
        in_specs += [spec, spec]
        args += list(rope_tabs)
    state_spec = pl.BlockSpec((None, 2, hp, dk, dv), lambda b, h, lg: (b, 0, h, 0, 0))
    if init_state is not None:
        in_specs.append(state_spec)
        args.append(init_state)
    out_specs = [pl.BlockSpec((t, wv), lambda b, h, lg: (b, h))]
    out_shape = [jax.ShapeDtypeStruct((n_seq * t, v_w), _BF16)]
    if want_final:
        out_specs.append(state_spec)
        out_shape.append(jax.ShapeDtypeStruct((n_seq, 2, nh, dk, dv), _F32))
    res = pl.pallas_call(
        functools.partial(_retention_both_kernel, rope=rope_tabs is not None,
                          has_init=init_state is not None, has_final=want_final, n_heads=nh, hp=hp),
        grid_spec=pltpu.PrefetchScalarGridSpec(
            num_scalar_prefetch=1,
            grid=(n_seq, nh // hp),
            in_specs=in_specs,
            out_specs=out_specs,
            scratch_shapes=[pltpu.VMEM((hp, dk, dv), _F32), pltpu.VMEM((hp, dk, dv), _F32),
                            pltpu.VMEM((t, wv), _F32), pltpu.VMEM((t, wv), _F32)],
        ),
        out_shape=out_shape,
        compiler_params=_params("parallel", "parallel"),
        name="retention_both",
    )(log_gamma.reshape(-1), *args)
    return (res[0], res[1]) if want_final else (res[0], None)


def _retention(proj, decay_logit, *, row0, n_seq, t, rope_tabs, init_state, want_final):
    log_gamma = jax.nn.log_sigmoid(decay_logit.astype(_F32))
    common = dict(row0=row0, n_seq=n_seq, t=t, rope_tabs=rope_tabs, init_state=init_state)
    if t <= RET_TT:
        return _retention_both(proj, log_gamma, want_final=want_final, **common)
    assert not want_final, "final states are only produced by the resident-sequence kernel"
    y_f = _retention_direction(proj, log_gamma, backward=False, prev=None, out_dtype=_F32, **common)
    y = _retention_direction(proj, log_gamma, backward=True, prev=y_f, out_dtype=_BF16, **common)
    return y, None


def _qkv_post_kernel(*refs, n_q, n_kv, hd, rope, want_cache):
    it = iter(refs)
    x_ref, qn_ref, kn_ref = next(it), next(it), next(it)
    if rope:
        cos_ref, sin_ref = next(it), next(it)
    q_ref, k_ref, v_ref = next(it), next(it), next(it)
    if want_cache:
        ck_ref, cv_ref = next(it), next(it)

    def head(col, g_ref):
        x = x_ref[:, col * hd:(col + 1) * hd]
        return x * lax.rsqrt(jnp.mean(x * x, axis=-1, keepdims=True) + RMS_EPS) * g_ref[...]

    def rotate(x):
        if not rope:
            return x
        return x * cos_ref[...] + pltpu.roll(x, hd // 2, 1) * sin_ref[...]

    for hh in range(n_q):
        q_ref[:, hh * hd:(hh + 1) * hd] = rotate(head(hh, qn_ref)).astype(q_ref.dtype)
    for hh in range(n_kv):
        kn = head(n_q + hh, kn_ref)
        v = x_ref[:, (n_q + n_kv + hh) * hd:(n_q + n_kv + hh + 1) * hd]
        if want_cache:
            ck_ref[0, hh] = kn
            cv_ref[0, hh] = v
        k_ref[0, hh] = rotate(kn).astype(k_ref.dtype)
        v_ref[0, hh] = v.astype(v_ref.dtype)


def _qkv_post(qkv, q_norm, k_norm, *, row0, n_seq, t, rope_tabs, want_cache):
    n_q, n_kv = ATTN_HEADS, KV_HEADS
    hd = qkv.shape[1] // (n_q + 2 * n_kv)
    tb = _tile(t, QKV_TB, SUBLANES)
    nt = t // tb
    rb0 = row0 // tb
    in_specs = [pl.BlockSpec((tb, qkv.shape[1]), lambda i: (rb0 + i, 0)),
                pl.BlockSpec((1, hd), lambda i: (0, 0)),
                pl.BlockSpec((1, hd), lambda i: (0, 0))]
    args = [qkv, q_norm.reshape(1, hd), k_norm.reshape(1, hd)]
    if rope_tabs is not None:
        spec = pl.BlockSpec((tb, hd), lambda i: (i % nt, 0))
        in_specs += [spec, spec]
        args += list(rope_tabs)
    head_spec = pl.BlockSpec((1, n_kv, tb, hd), lambda i: (i // nt, 0, i % nt, 0))
    out_specs = [pl.BlockSpec((tb, n_q * hd), lambda i: (i, 0)), head_spec, head_spec]
    out_shape = [jax.ShapeDtypeStruct((n_seq * t, n_q * hd), _BF16),
                 jax.ShapeDtypeStruct((n_seq, n_kv, t, hd), _BF16),
                 jax.ShapeDtypeStruct((n_seq, n_kv, t, hd), _BF16)]
    if want_cache:
        out_specs += [head_spec, head_spec]
        out_shape += [jax.ShapeDtypeStruct((n_seq, n_kv, t, hd), _F32)] * 2
    return pl.pallas_call(
        functools.partial(_qkv_post_kernel, n_q=n_q, n_kv=n_kv, hd=hd,
                          rope=rope_tabs is not None, want_cache=want_cache),
        grid=(n_seq * nt,),
        in_specs=in_specs,
        out_specs=out_specs,
        out_shape=out_shape,
        compiler_params=_params("parallel"),
        name="qkv_norm_rope",
    )(*args)


def _attention_kernel(q_ref, k_ref, v_ref, o_ref, *, group, hd):
    for g in range(group):
        cols = slice(g * hd, (g + 1) * hd)
        s = lax.dot_general(q_ref[:, cols], k_ref[...], (((1,), (1,)), ((), ())),
                            preferred_element_type=_F32)
        e = jnp.exp((s - jnp.max(s, axis=-1, keepdims=True)) * (hd ** -0.5))
        inv = 1.0 / jnp.sum(e, axis=-1, keepdims=True)
        o_ref[:, cols] = (_dot(e.astype(_BF16), v_ref[...]) * inv).astype(o_ref.dtype)


def _attention(q, k, v, *, n_seq, t):
    n_kv, n_keys, hd = k.shape[1], k.shape[2], k.shape[3]
    group = ATTN_HEADS // n_kv
    tq = _tile(t, ATTN_TQ if n_keys > 1024 else 256, SUBLANES)
    nt = t // tq
    q_spec = pl.BlockSpec((tq, group * hd), lambda b, kh, i: (b * nt + i, kh))
    kv_spec = pl.BlockSpec((None, None, n_keys, hd), lambda b, kh, i: (b, kh, 0, 0))
    return pl.pallas_call(
        functools.partial(_attention_kernel, group=group, hd=hd),
        grid=(n_seq, n_kv, nt),
        in_specs=[q_spec, kv_spec, kv_spec],
        out_specs=q_spec,
        out_shape=jax.ShapeDtypeStruct(q.shape, _BF16),
        compiler_params=_params("parallel", "parallel", "parallel"),
        name="attention",
    )(q, k, v)


def _router_kernel(h_ref, w_ref, mi_ref, mf_ref, cnt_ref, run_ref, *, n_exp):
    @pl.when(pl.program_id(0) == 0)
    def _():
        run_ref[...] = jnp.zeros_like(run_ref)

    tb = h_ref.shape[0]
    logits = jnp.dot(h_ref[...], w_ref[...], precision=lax.Precision.HIGHEST,
                     preferred_element_type=_F32)
    lane = lax.broadcasted_iota(jnp.int32, logits.shape, 1).astype(_F32)
    neg = -jnp.inf
    l1 = jnp.where(lane < n_exp, logits, neg)
    m1 = jnp.max(l1, axis=-1, keepdims=True)
    i1 = jnp.min(jnp.where(l1 == m1, lane, float(LANES)), axis=-1, keepdims=True)
    l2 = jnp.where(lane == i1, neg, l1)
    m2 = jnp.max(l2, axis=-1, keepdims=True)
    i2 = jnp.min(jnp.where(l2 == m2, lane, float(LANES)), axis=-1, keepdims=True)
    e2 = jnp.exp(m2 - m1)
    g1 = 1.0 / (1.0 + e2)
    g2 = e2 / (1.0 + e2)
    oh1 = (lane == i1).astype(_F32)
    oh2 = (lane == i2).astype(_F32)
    sel = oh1 + oh2
    r = lax.broadcasted_iota(jnp.int32, (tb, tb), 0)
    c = lax.broadcasted_iota(jnp.int32, (tb, tb), 1)
    before = _dot((c < r).astype(_BF16), sel.astype(_BF16)) + run_ref[...]
    rank1 = jnp.sum(before * oh1, axis=-1, keepdims=True)
    rank2 = jnp.sum(before * oh2, axis=-1, keepdims=True)
    run_ref[...] += jnp.sum(sel, axis=0, keepdims=True)
    cnt_ref[...] = run_ref[...].astype(jnp.int32)
    mi = jnp.where(lane == 0, i1, jnp.where(lane == 1, i2,
                   jnp.where(lane == 2, rank1, jnp.where(lane == 3, rank2, 0.0))))
    mi_ref[...] = mi.astype(jnp.int32)
    mf_ref[...] = jnp.where(lane == 0, g1, jnp.where(lane == 1, g2, 0.0))


def _router(h32, w_router):
    n, d = h32.shape
    n_exp = w_router.shape[1]
    w = jnp.zeros((d, LANES), _F32).at[:, :n_exp].set(w_router)
    tb = _tile(n, ROUTER_TB, SUBLANES)
    row = pl.BlockSpec((tb, LANES), lambda i: (i, 0))
    return pl.pallas_call(
        functools.partial(_router_kernel, n_exp=n_exp),
        grid=(n // tb,),
        in_specs=[pl.BlockSpec((tb, d), lambda i: (i, 0)), pl.BlockSpec((d, LANES), lambda i: (0, 0))],
        out_specs=[row, row, pl.BlockSpec((1, LANES), lambda i: (0, 0))],
        out_shape=[jax.ShapeDtypeStruct((n, LANES), jnp.int32),
                   jax.ShapeDtypeStruct((n, LANES), _F32),
                   jax.ShapeDtypeStruct((1, LANES), jnp.int32)],
        scratch_shapes=[pltpu.VMEM((1, LANES), _F32)],
        compiler_params=_params("arbitrary"),
        name="moe_router",
    )(h32, w)


def _invert_kernel(pos_ref, inv_ref, *, n_tok):
    def clear(r, carry):
        inv_ref[r] = 0
        return carry

    lax.fori_loop(0, inv_ref.shape[0], clear, 0, unroll=ROW_LOOP_UNROLL)

    def put(t, carry):
        for slot in range(TOP_K):
            inv_ref[pos_ref[slot * n_tok + t]] = t
        return carry

    lax.fori_loop(0, n_tok, put, 0, unroll=ROW_LOOP_UNROLL)


def _invert(pos, n_rows, n_tok):
    return pl.pallas_call(
        functools.partial(_invert_kernel, n_tok=n_tok),
        in_specs=[pl.BlockSpec(memory_space=pltpu.SMEM)],
        out_specs=pl.BlockSpec(memory_space=pltpu.SMEM),
        out_shape=jax.ShapeDtypeStruct((n_rows,), jnp.int32),
        name="moe_invert",
    )(pos)


def _dispatch_kernel(inv_ref, live_ref, h_hbm, o_ref, buf, sem):
    c = pl.program_id(0)
    ch = buf.shape[1]
    slot = c % 2
    live = live_ref[c]

    def row_copy(s, k, src):
        return pltpu.make_async_copy(h_hbm.at[pl.ds(src, 1), :], buf.at[s, pl.ds(k, 1), :], sem.at[s])

    def start_chunk(cc, s):
        @pl.when(live_ref[cc] > 0)
        def _():
            def issue(k, carry):
                row_copy(s, k, inv_ref[cc * ch + k]).start()
                return carry

            lax.fori_loop(0, ch, issue, 0, unroll=ROW_LOOP_UNROLL)

    @pl.when(c == 0)
    def _():
        start_chunk(0, 0)

    @pl.when(c + 1 < pl.num_programs(0))
    def _():
        start_chunk(c + 1, 1 - slot)

    @pl.when(live == 0)
    def _():
        o_ref[...] = jnp.zeros_like(o_ref)

    @pl.when(live > 0)
    def _():
        def drain(k, carry):
            row_copy(slot, k, 0).wait()
            return carry

        lax.fori_loop(0, ch, drain, 0, unroll=ROW_LOOP_UNROLL)
        rows = lax.broadcasted_iota(jnp.int32, (ch, 1), 0)
        o_ref[...] = jnp.where(rows < live, buf[slot], 0.0).astype(o_ref.dtype)


def _dispatch(h32, inv, chunk_live, chunk):
    n, d = h32.shape
    n_rows = inv.shape[0]
    return pl.pallas_call(
        _dispatch_kernel,
        grid_spec=pltpu.PrefetchScalarGridSpec(
            num_scalar_prefetch=2,
            grid=(n_rows // chunk,),
            in_specs=[pl.BlockSpec(memory_space=pl.ANY)],
            out_specs=pl.BlockSpec((chunk, d), lambda c, inv, live: (c, 0)),
            scratch_shapes=[pltpu.VMEM((2, chunk, d), _F32), pltpu.SemaphoreType.DMA((2,))],
        ),
        out_shape=jax.ShapeDtypeStruct((n_rows, d), _BF16),
        compiler_params=_params("arbitrary"),
        name="moe_dispatch",
    )(inv, chunk_live, h32)


def _group_swiglu_kernel(te_ref, ns_ref, x_ref, w1_ref, w3_ref, o_ref, *, sub):
    del te_ref
    n_sub = ns_ref[pl.program_id(0)]
    full = x_ref.shape[0] // sub

    def act(x):
        a = _dot(x, w1_ref[...].astype(_BF16))
        return (_silu(a) * _dot(x, w3_ref[...].astype(_BF16))).astype(o_ref.dtype)

    @pl.when(n_sub == full)
    def _():
        o_ref[...] = act(x_ref[...])

    @pl.when(n_sub < full)
    def _():
        o_ref[...] = jnp.zeros_like(o_ref)

        def body(s, carry):
            rows = pl.ds(pl.multiple_of(s * sub, sub), sub)
            o_ref[rows, :] = act(x_ref[rows, :])
            return carry

        lax.fori_loop(0, n_sub, body, 0)


def _group_down_kernel(te_ref, ns_ref, x_ref, w_ref, o_ref, *, sub):
    del te_ref
    n_sub = ns_ref[pl.program_id(0)]
    full = x_ref.shape[0] // sub

    @pl.when(pl.program_id(2) == 0)
    def _():
        o_ref[...] = jnp.zeros_like(o_ref)

    @pl.when(n_sub == full)
    def _():
        o_ref[...] += _dot(x_ref[...], w_ref[...].astype(_BF16))

    @pl.when(n_sub < full)
    def _():
        def body(s, carry):
            rows = pl.ds(pl.multiple_of(s * sub, sub), sub)
            o_ref[rows, :] += _dot(x_ref[rows, :], w_ref[...].astype(_BF16))
            return carry

        lax.fori_loop(0, n_sub, body, 0)


def _grouped_experts(xg, tile_expert, tile_nsub, w1, w3, w2, *, tm, sub):
    r, d = xg.shape
    f = w1.shape[2]
    nt = r // tm
    tn = _tile(f, MOE_TN, LANES)

    def live(idx, ns, i, parked):
        return jnp.where(ns[i] > 0, idx, parked)

    nj = f // tn
    w_up = pl.BlockSpec((None, d, tn), lambda i, j, te, ns: (te[i], 0, live(j, ns, i, nj - 1)))
    a = pl.pallas_call(
        functools.partial(_group_swiglu_kernel, sub=sub),
        grid_spec=pltpu.PrefetchScalarGridSpec(
            num_scalar_prefetch=2,
            grid=(nt, nj),
            in_specs=[pl.BlockSpec((tm, d), lambda i, j, te, ns: (live(i, ns, i, 0), 0),
                                   pipeline_mode=pl.Buffered(1)), w_up, w_up],
            out_specs=pl.BlockSpec((tm, tn), lambda i, j, te, ns: (i, j)),
        ),
        out_shape=jax.ShapeDtypeStruct((r, f), _BF16),
        compiler_params=pltpu.CompilerParams(dimension_semantics=("parallel", "parallel"),
                                             vmem_limit_bytes=V7X_VMEM_BYTES * 15 // 16),
        name="moe_swiglu_up",
    )(tile_expert, tile_nsub, xg, w1, w3)
    tn2, tk = _tile(d, MOE_DOWN_TN, LANES), _tile(f, MOE_TK, LANES)
    nj2, nk = d // tn2, f // tk
    return pl.pallas_call(
        functools.partial(_group_down_kernel, sub=sub),
        grid_spec=pltpu.PrefetchScalarGridSpec(
            num_scalar_prefetch=2,
            grid=(nt, nj2, nk),
            in_specs=[pl.BlockSpec((tm, tk), lambda i, j, k, te, ns:
                                   (live(i, ns, i, 0), live(k, ns, i, 0))),
                      pl.BlockSpec((None, tk, tn2), lambda i, j, k, te, ns:
                                   (te[i], live(k, ns, i, nk - 1), live(j, ns, i, nj2 - 1)))],
            out_specs=pl.BlockSpec((tm, tn2), lambda i, j, k, te, ns: (i, j)),
        ),
        out_shape=jax.ShapeDtypeStruct((r, d), _F32),
        compiler_params=_params("parallel", "parallel", "arbitrary"),
        name="moe_down",
    )(tile_expert, tile_nsub, a, w2)


def _combine_kernel(p1_ref, p2_ref, y_hbm, gate_ref, x_ref, g_ref, mod_ref, o_ref, buf, sem,
                    *, gate_idx):
    c = pl.program_id(0)
    ch = buf.shape[2]
    slot = c % 2

    def row_copy(s, e, k, src):
        return pltpu.make_async_copy(y_hbm.at[pl.ds(src, 1), :], buf.at[s, e, pl.ds(k, 1), :],
                                     sem.at[s, e])

    def start_chunk(cc, s):
        def issue(k, carry):
            row_copy(s, 0, k, p1_ref[cc * ch + k]).start()
            row_copy(s, 1, k, p2_ref[cc * ch + k]).start()
            return carry

        lax.fori_loop(0, ch, issue, 0, unroll=ROW_LOOP_UNROLL)

    @pl.when(c == 0)
    def _():
        start_chunk(0, 0)

    @pl.when(c + 1 < pl.num_programs(0))
    def _():
        start_chunk(c + 1, 1 - slot)

    def drain(k, carry):
        row_copy(slot, 0, k, 0).wait()
        row_copy(slot, 1, k, 0).wait()
        return carry

    lax.fori_loop(0, ch, drain, 0, unroll=ROW_LOOP_UNROLL)
    gates = gate_ref[...]
    f = gates[:, 0:1] * buf[slot, 0] + gates[:, 1:2] * buf[slot, 1]
    o_ref[...] = x_ref[...] + mod_ref[0, pl.ds(gate_idx, 1), :] * _rms(f, g_ref[...])


def _combine(y, pos1, pos2, gates, x, g_post, mods, gate_idx, group_of, chunk):
    n, d = x.shape
    row = pl.BlockSpec((chunk, d), lambda c, p1, p2: (c, 0))
    return pl.pallas_call(
        functools.partial(_combine_kernel, gate_idx=gate_idx),
        grid_spec=pltpu.PrefetchScalarGridSpec(
            num_scalar_prefetch=2,
            grid=(n // chunk,),
            in_specs=[pl.BlockSpec(memory_space=pl.ANY),
                      pl.BlockSpec((chunk, LANES), lambda c, p1, p2: (c, 0)),
                      row,
                      pl.BlockSpec((1, d), lambda c, p1, p2: (0, 0)),
                      pl.BlockSpec((1, N_MOD, d), lambda c, p1, p2: (group_of(c * chunk), 0, 0))],
            out_specs=row,
            scratch_shapes=[pltpu.VMEM((2, TOP_K, chunk, d), _F32),
                            pltpu.SemaphoreType.DMA((2, TOP_K))],
        ),
        out_shape=jax.ShapeDtypeStruct((n, d), _F32),
        compiler_params=_params("arbitrary"),
        name="moe_combine",
    )(pos1, pos2, y, gates, x, g_post.reshape(1, d), mods)


def _moe(h32, x, w_router, w1, w3, w2, g_post, mods, gate_idx, group_of):
    n, d = h32.shape
    n_exp = w_router.shape[1]
    tm = _tile(TOP_K * n, MOE_TM, SUBLANES)
    sub = _tile(tm, MOE_SUB, SUBLANES)
    meta_i, meta_f, counts = _router(h32, w_router)
    cnt = counts[0, :n_exp]
    padded = (cnt + tm - 1) // tm * tm
    ends = jnp.cumsum(padded)
    offs = ends - padded
    pos1 = (offs[meta_i[:, 0]] + meta_i[:, 2]).astype(jnp.int32)
    pos2 = (offs[meta_i[:, 1]] + meta_i[:, 3]).astype(jnp.int32)
    n_rows = TOP_K * n + n_exp * tm

    def expert_of(start):
        return jnp.minimum(jnp.sum(ends[None, :] <= start[:, None], axis=1), n_exp - 1)

    def live_rows(start, width):
        e = expert_of(start)
        return jnp.clip(cnt[e] - (start - offs[e]), 0, width).astype(jnp.int32)

    tile_start = jnp.arange(n_rows // tm, dtype=jnp.int32) * tm
    tile_expert = expert_of(tile_start).astype(jnp.int32)
    tile_nsub = (live_rows(tile_start, tm) + sub - 1) // sub
    chunk_live = live_rows(jnp.arange(n_rows // sub, dtype=jnp.int32) * sub, sub)
    inv = _invert(jnp.concatenate([pos1, pos2]), n_rows, n)
    xg = _dispatch(h32, inv, chunk_live, sub)
    y = _grouped_experts(xg, tile_expert, tile_nsub, w1, w3, w2, tm=tm, sub=sub)
    return _combine(y, pos1, pos2, meta_f, x, g_post, mods, gate_idx, group_of,
                    _tile(n, MOE_SUB, SUBLANES))


def kernel(x_prompt, x_sample, state_ret, cache_k, cache_v, c, c_ctx, w_mod, b_mod, norm_g,
           ret_w_in, ret_decay_logit, ret_w_out, attn_w_qkv, attn_q_norm, attn_k_norm, attn_w_o,
           ffn_w1, ffn_w3, ffn_w2, moe_router, moe_w1, moe_w3, moe_w2):
    bp, sp, d = x_prompt.shape
    bs, ss, _ = x_sample.shape
    n_p, n_s = bp * sp, bs * ss
    depth = w_mod.shape[0]

    def group_of(row):
        return jnp.where(row < n_p, 0, 1 + (row - n_p) // ss)

    x = jnp.concatenate([x_prompt.reshape(n_p, d), x_sample.reshape(n_s, d)], axis=0)
    mods = _modulation(jnp.concatenate([c_ctx[None, :], c], axis=0), w_mod, b_mod)

    new_ret, new_k, new_v = [], [], []
    (h,) = _post(x, group_of, nxt=(norm_g[0, 0], mods[0], 0, 1))
    for i in range(depth):
        j = i // 2
        md = mods[i]
        last = i == depth - 1
        if i % 2 == 0:
            proj = _matmul(h, ret_w_in[j].astype(_BF16), _F32, name="ret_in_proj")
            dk = proj.shape[1] // (8 * RET_HEADS)
            y_p, st = _retention(proj, ret_decay_logit[j], row0=0, n_seq=bp, t=sp, rope_tabs=None,
                                 init_state=None, want_final=True)
            y_s, _ = _retention(proj, ret_decay_logit[j], row0=n_p, n_seq=bs, t=ss,
                                rope_tabs=_rope_tables(ss, dk), init_state=state_ret[:, j],
                                want_final=False)
            new_ret.append(st)
            mix = _matmul(jnp.concatenate([y_p, y_s], axis=0), ret_w_out[j].astype(_BF16), _F32,
                          name="ret_out_proj")
        else:
            qkv = _matmul(h, attn_w_qkv[j].astype(_BF16), _F32, name="attn_qkv_proj")
            hd = qkv.shape[1] // (ATTN_HEADS + 2 * KV_HEADS)
            q_p, k_p, v_p, ck, cv = _qkv_post(qkv, attn_q_norm[j], attn_k_norm[j], row0=0, n_seq=bp,
                                              t=sp, rope_tabs=None, want_cache=True)
            cos, sin = _rope_tables(ss, hd)
            tabs = (jnp.concatenate([cos, cos], axis=-1), jnp.concatenate([-sin, sin], axis=-1))
            q_s, k_s, v_s = _qkv_post(qkv, attn_q_norm[j], attn_k_norm[j], row0=n_p, n_seq=bs, t=ss,
                                      rope_tabs=tabs, want_cache=False)
            new_k.append(ck)
            new_v.append(cv)
            o_p = _attention(q_p, k_p, v_p, n_seq=bp, t=sp)
            o_s = _attention(q_s,
                             jnp.concatenate([cache_k[:, j].astype(_BF16), k_s], axis=2),
                             jnp.concatenate([cache_v[:, j].astype(_BF16), v_s], axis=2),
                             n_seq=bs, t=ss)
            mix = _matmul(jnp.concatenate([o_p, o_s], axis=0), attn_w_o[j].astype(_BF16), _F32,
                          name="attn_out_proj")
        moe_layer = i % 2 == 1
        x, h = _post(x, group_of, update=(mix, norm_g[i, 1], md, 2), nxt=(norm_g[i, 2], md, 3, 4),
                     h_dtype=_F32 if moe_layer else _BF16)
        if not moe_layer:
            f = ffn_w1.shape[2]
            fp = -(-f // FFN_PAD) * FFN_PAD
            pad_c = lambda w: jnp.pad(w.astype(_BF16), ((0, 0), (0, fp - f)))
            up = _swiglu_up(h, pad_c(ffn_w1[j]), pad_c(ffn_w3[j]))
            w2 = jnp.pad(ffn_w2[j].astype(_BF16), ((0, fp - f), (0, 0)))
            ff = _matmul(up, w2, _F32, tk_target=fp // 4, name="ffn_down")
            nxt = None if last else (norm_g[i + 1, 0], mods[i + 1], 0, 1)
            res = _post(x, group_of, update=(ff, norm_g[i, 3], md, 5), nxt=nxt)
            x = res[0]
            h = None if last else res[1]
        else:
            x = _moe(h, x, moe_router[j], moe_w1[j], moe_w3[j], moe_w2[j], norm_g[i, 3], md, 5,
                     group_of)
            if not last:
                (h,) = _post(x, group_of, nxt=(norm_g[i + 1, 0], mods[i + 1], 0, 1))

    return (x[:n_p].reshape(bp, sp, d), x[n_p:].reshape(bs, ss, d),
            jnp.stack(new_ret, axis=1), jnp.stack(new_k, axis=1), jnp.stack(new_v, axis=1))
```
